```python
import jax, jax.numpy as jnp
from jax import lax
import numpy as np

D_MODEL = 1024
BATCH = 8
SEQ = 8192
DEPTH = 1

CTX_LEN = 256
GRID_W = 64
N_HEADS = 8
HEAD_DIM = D_MODEL // N_HEADS
HGRN_WIDTH = N_HEADS * HEAD_DIM
CHUNK = 64
POOL_WINDOWS = (2, 4, 8, 16)
POOL_GROUP = 128
POOL_WIDTH = POOL_GROUP * len(POOL_WINDOWS)
N_EXPERTS = 16
CAPACITY_FACTOR = 2
EXPERT_FF = 1024
EPS = 1e-6

Q_OFF = 0
I_OFF = Q_OFF + HGRN_WIDTH
FF_OFF = I_OFF + HGRN_WIDTH
FB_OFF = FF_OFF + HGRN_WIDTH
OG_OFF = FB_OFF + HGRN_WIDTH
P_OFF = OG_OFF + HGRN_WIDTH
GA_OFF = P_OFF + POOL_WIDTH
GB_OFF = GA_OFF + D_MODEL
IN_COLS = GB_OFF + D_MODEL

kernel_name = "hybrid_hgrn2_pool_ec_moe_dit"


def rms_norm(x, g):
    xf = x.astype(jnp.float32)
    y = xf * lax.rsqrt(jnp.mean(xf * xf, axis=-1, keepdims=True) + EPS)
    return y.astype(x.dtype) * g


def modulate(x, shift, scale):
    return x * (1 + scale) + shift


def adaln(cvec, w_mod, b_mod):
    return jnp.split(jax.nn.silu(cvec) @ w_mod + b_mod, 6, axis=-1)


def to_heads(z):
    return z.astype(jnp.float32).reshape(z.shape[0], z.shape[1], N_HEADS, HEAD_DIM)


def hgrn_forget(z, lb):
    f = lb + (1.0 - lb) * jax.nn.sigmoid(z.astype(jnp.float32))
    return to_heads(1.0 - f), to_heads(jnp.log(f))


def flip(a):
    return a[:, ::-1]


def chunked_recurrence(q, k, v, logf, s0):
    bn, length, h, _ = q.shape
    dv = v.shape[-1]
    n = length // CHUNK

    def to_chunks(a):
        return a.reshape(bn, n, CHUNK, h, a.shape[-1]).transpose(1, 0, 3, 2, 4)

    causal = jnp.tril(jnp.ones((CHUNK, CHUNK), dtype=bool))

    def step(state, inp):
        qc, kc, vc, gc = inp
        b = jnp.cumsum(gc, axis=2)
        rel = jnp.where(causal[:, :, None], b[:, :, :, None, :] - b[:, :, None, :, :], -jnp.inf)
        scores = jnp.einsum('bhtd,bhsd,bhtsd->bhts', qc, kc, jnp.exp(rel))
        o = (jnp.einsum('bhts,bhsv->bhtv', scores, vc)
             + jnp.einsum('bhtd,bhdv->bhtv', qc * jnp.exp(b), state))
        b_end = b[:, :, -1:, :]
        state = (jnp.exp(b_end[:, :, 0, :])[..., None] * state
                 + jnp.einsum('bhsd,bhsv->bhdv', kc * jnp.exp(b_end - b), vc))
        return state, o

    s_fin, o = lax.scan(step, s0, (to_chunks(q), to_chunks(k), to_chunks(v), to_chunks(logf)))
    return o.transpose(1, 0, 3, 2, 4).reshape(bn, length, h, dv), s_fin


def final_state(k, v, logf, s0):
    b = jnp.cumsum(logf, axis=1)
    b_end = b[:, -1]
    return (jnp.exp(b_end)[..., None] * s0
            + jnp.einsum('blhd,blhv->bhdv', k * jnp.exp(b_end[:, None] - b), v))


def window_bounds(n, w):
    pos = jnp.arange(n)
    lo = jnp.clip(pos - w // 2, 0, n - 1)
    hi = jnp.clip(pos + w // 2 - 1, 0, n - 1) + 1
    return lo, hi


def centred_pool_residual(p, rows, cols, w):
    bn, length, ch = p.shape
    g = p.astype(jnp.float32).reshape(bn, rows, cols, ch)
    sat = jnp.pad(jnp.cumsum(jnp.cumsum(g, axis=1), axis=2), ((0, 0), (1, 0), (1, 0), (0, 0)))
    r0, r1 = window_bounds(rows, w)
    c0, c1 = window_bounds(cols, w)

    def corner(ri, ci):
        return sat[:, ri][:, :, ci]

    total = corner(r1, c1) - corner(r0, c1) - corner(r1, c0) + corner(r0, c0)
    count = ((r1 - r0)[:, None] * (c1 - c0)[None, :]).astype(jnp.float32)[None, :, :, None]
    return (total / count - g).reshape(bn, length, ch).astype(p.dtype)


def pool_branch(p, rows, cols, w_pool, pool_scale):
    outs = []
    for j, w in enumerate(POOL_WINDOWS):
        pj = p[..., j * POOL_GROUP:(j + 1) * POOL_GROUP]
        outs.append(centred_pool_residual(pj, rows, cols, w) @ w_pool[j])
    return jnp.concatenate(outs, axis=-1) * pool_scale


def merge_branches(o, z, rows, cols, hgrn_norm_g, w_a, w_pool, pool_scale, w_b, w_out):
    bn, length = z.shape[:2]
    o_gated = rms_norm(o, hgrn_norm_g).reshape(bn, length, HGRN_WIDTH).astype(z.dtype) * jax.nn.silu(z[..., OG_OFF:P_OFF])
    y_a = o_gated @ w_a
    y_b = pool_branch(z[..., P_OFF:GA_OFF], rows, cols, w_pool, pool_scale) @ w_b
    y = jax.nn.sigmoid(z[..., GA_OFF:GB_OFF]) * y_a + jax.nn.sigmoid(z[..., GB_OFF:IN_COLS]) * y_b
    return y @ w_out


def expert_choice_moe(u, w_router, w_e_gate, w_e_up, w_e_down):
    bn, n, _ = u.shape
    cap = CAPACITY_FACTOR * n // N_EXPERTS
    probs = jax.nn.softmax((u @ w_router).astype(jnp.float32), axis=-1)
    vals, idx = lax.top_k(jnp.swapaxes(probs, 1, 2), cap)
    bidx = jnp.arange(bn)[:, None, None]
    xg = u[bidx, idx]
    hid = jax.nn.silu(jnp.einsum('becd,edf->becf', xg, w_e_gate)) * jnp.einsum('becd,edf->becf', xg, w_e_up)
    y = jnp.einsum('becf,efd->becd', hid, w_e_down) * vals[..., None].astype(u.dtype)
    return jnp.zeros_like(u).at[bidx, idx].add(y)


def trunk_layer(hx, hc, c, c_ctx, lb_f, lb_b, w_mod, b_mod, norm1_g, w_in, hgrn_norm_g,
                w_a, w_pool, pool_scale, w_b, w_out, norm2_g, w_router, w_e_gate, w_e_up,
                w_e_down, rows, update_ctx):
    sh1x, sc1x, g1x, sh2x, sc2x, g2x = adaln(c[:, None, :], w_mod, b_mod)
    sh1c, sc1c, g1c, sh2c, sc2c, g2c = adaln(c_ctx, w_mod, b_mod)

    ux = modulate(rms_norm(hx, norm1_g), sh1x, sc1x)
    uc = modulate(rms_norm(hc, norm1_g), sh1c, sc1c)
    zx = ux @ w_in
    if update_ctx:
        zc = uc @ w_in
        zc_ifb = zc[..., I_OFF:OG_OFF]
    else:
        zc_ifb = uc @ w_in[:, I_OFF:OG_OFF]

    s0 = jnp.zeros((hc.shape[0], N_HEADS, HEAD_DIM, HEAD_DIM), jnp.float32)
    ic = to_heads(zc_ifb[..., :HGRN_WIDTH])
    kcf, gcf = hgrn_forget(zc_ifb[..., HGRN_WIDTH:2 * HGRN_WIDTH], lb_f)
    kcb, gcb = hgrn_forget(zc_ifb[..., 2 * HGRN_WIDTH:3 * HGRN_WIDTH], lb_b)
    if update_ctx:
        qc = to_heads(jax.nn.silu(zc[..., Q_OFF:I_OFF]))
        oc_f, sc_f = chunked_recurrence(qc, kcf, ic, gcf, s0)
        oc_b, sc_b = chunked_recurrence(flip(qc), flip(kcb), flip(ic), flip(gcb), s0)
        oc = oc_f + flip(oc_b)
    else:
        sc_f = final_state(kcf, ic, gcf, s0)
        sc_b = final_state(flip(kcb), flip(ic), flip(gcb), s0)

    qx = to_heads(jax.nn.silu(zx[..., Q_OFF:I_OFF]))
    ix = to_heads(zx[..., I_OFF:FF_OFF])
    kxf, gxf = hgrn_forget(zx[..., FF_OFF:FB_OFF], lb_f)
    kxb, gxb = hgrn_forget(zx[..., FB_OFF:OG_OFF], lb_b)
    ox_f, _ = chunked_recurrence(qx, kxf, ix, gxf, sc_f)
    ox_b, _ = chunked_recurrence(flip(qx), flip(kxb), flip(ix), flip(gxb), sc_b)
    ox = ox_f + flip(ox_b)

    new_hx = hx + g1x * merge_branches(ox, zx, rows, GRID_W, hgrn_norm_g, w_a, w_pool, pool_scale, w_b, w_out)
    if update_ctx:
        hc = hc + g1c * merge_branches(oc, zc, 1, hc.shape[1], hgrn_norm_g, w_a, w_pool, pool_scale, w_b, w_out)

    vx = modulate(rms_norm(new_hx, norm2_g), sh2x, sc2x)
    new_hx = new_hx + g2x * expert_choice_moe(vx, w_router, w_e_gate, w_e_up, w_e_down)
    if update_ctx:
        vc = modulate(rms_norm(hc, norm2_g), sh2c, sc2c)
        hc = hc + g2c * expert_choice_moe(vc, w_router, w_e_gate, w_e_up, w_e_down)
    return new_hx, hc


def setup_inputs(seed: int = 0) -> dict:
    key = jax.random.key(seed)
    ks = jax.random.split(key, 24)
    nrm = lambda k, shape, s: jax.random.normal(k, shape, jnp.float32) * s
    d = D_MODEL
    return {
        "x": nrm(ks[0], (BATCH, SEQ, d), 1.0),
        "c": nrm(ks[1], (BATCH, d), 1.0),
        "ctx": nrm(ks[2], (BATCH, CTX_LEN, d), 1.0),
        "c_ctx": nrm(ks[3], (d,), 1.0),
        "lb_logits": nrm(ks[4], (2, DEPTH + 1, HGRN_WIDTH), 0.5),
        "w_mod": nrm(ks[5], (DEPTH, d, 6 * d), 0.5 * d ** -0.5),
        "b_mod": nrm(ks[6], (DEPTH, 6 * d), 0.02),
        "norm1_g": 1.0 + nrm(ks[7], (DEPTH, d), 0.02),
        "w_in": nrm(ks[8], (DEPTH, d, IN_COLS), d ** -0.5),
        "hgrn_norm_g": 1.0 + nrm(ks[9], (DEPTH, HEAD_DIM), 0.02),
        "w_a": nrm(ks[10], (DEPTH, HGRN_WIDTH, d), HGRN_WIDTH ** -0.5),
        "w_pool": nrm(ks[11], (DEPTH, len(POOL_WINDOWS), POOL_GROUP, POOL_GROUP), POOL_GROUP ** -0.5),
        "pool_scale": 1.0 + nrm(ks[12], (DEPTH, POOL_WIDTH), 0.1),
        "w_b": nrm(ks[13], (DEPTH, POOL_WIDTH, d), POOL_WIDTH ** -0.5),
        "w_out": nrm(ks[14], (DEPTH, d, d), d ** -0.5),
        "norm2_g": 1.0 + nrm(ks[15], (DEPTH, d), 0.02),
        "w_router": nrm(ks[16], (DEPTH, d, N_EXPERTS), d ** -0.5),
        "w_e_gate": nrm(ks[17], (DEPTH, N_EXPERTS, d, EXPERT_FF), d ** -0.5),
        "w_e_up": nrm(ks[18], (DEPTH, N_EXPERTS, d, EXPERT_FF), d ** -0.5),
        "w_e_down": nrm(ks[19], (DEPTH, N_EXPERTS, EXPERT_FF, d), EXPERT_FF ** -0.5),
        "final_g": 1.0 + nrm(ks[20], (d,), 0.02),
    }


def reference(x, c, ctx, c_ctx, lb_logits, w_mod, b_mod, norm1_g, w_in, hgrn_norm_g, w_a,
              w_pool, pool_scale, w_b, w_out, norm2_g, w_router, w_e_gate, w_e_up, w_e_down,
              final_g):
    rows = x.shape[1] // GRID_W
    lb = jnp.cumsum(jax.nn.softmax(lb_logits.astype(jnp.float32), axis=1), axis=1)
    hx, hc = x, ctx
    for l in range(DEPTH):
        hx, hc = trunk_layer(hx, hc, c, c_ctx, lb[0, l], lb[1, l], w_mod[l], b_mod[l], norm1_g[l],
                             w_in[l], hgrn_norm_g[l], w_a[l], w_pool[l], pool_scale[l], w_b[l],
                             w_out[l], norm2_g[l], w_router[l], w_e_gate[l], w_e_up[l], w_e_down[l],
                             rows, l < DEPTH - 1)
    return rms_norm(hx, final_g)
```

```python
import functools

import numpy as np
import jax
import jax.numpy as jnp
from jax import lax
from jax.experimental import pallas as pl
from jax.experimental.pallas import tpu as pltpu

F32 = jnp.float32
BF16 = jnp.bfloat16
I32 = jnp.int32

D_MODEL = 1024
N_HEADS = 8
HEAD_DIM = 128
HGRN_WIDTH = N_HEADS * HEAD_DIM
GRID_W = 64
POOL_WINDOWS = (2, 4, 8, 16)
POOL_GROUP = 128
POOL_WIDTH = POOL_GROUP * len(POOL_WINDOWS)
N_EXPERTS = 16
CAPACITY_FACTOR = 2
EXPERT_FF = 1024
EPS = 1e-6

Q_OFF = 0
I_OFF = Q_OFF + HGRN_WIDTH
FF_OFF = I_OFF + HGRN_WIDTH
FB_OFF = FF_OFF + HGRN_WIDTH
OG_OFF = FB_OFF + HGRN_WIDTH
P_OFF = OG_OFF + HGRN_WIDTH
GA_OFF = P_OFF + POOL_WIDTH
GB_OFF = GA_OFF + D_MODEL
IN_COLS = GB_OFF + D_MODEL

MOD_ROWS = 16
TOKEN_TILE = 256
HGRN_CHUNK = 128
HGRN_BLOCK = 32
HGRN_SAFE_DECAY = 80.0
POOL_HALO = 8
POOL_TILE = 256
SLOT_SHIFT = 6
SLOT_WINDOW = 1 << SLOT_SHIFT
FFN_ROWS = 256
VMEM_LIMIT = 56 * 1024 * 1024


def _dot(a, b):
    return jnp.dot(a, b, preferred_element_type=F32)


def _dot_nt(a, b):
    return lax.dot_general(a, b, (((1,), (1,)), ((), ())), preferred_element_type=F32)


def _dot_tn(a, b):
    return lax.dot_general(a, b, (((0,), (0,)), ((), ())), preferred_element_type=F32)


def _rms(x, g):
    ms = jnp.mean(x * x, axis=-1, keepdims=True)
    return x * lax.rsqrt(ms + EPS) * g


def _silu(z):
    return z * jax.nn.sigmoid(z)


def _lower_bound(a0, a1):
    m = jnp.maximum(a0, a1)
    e0 = jnp.exp(a0 - m)
    e1 = jnp.exp(a1 - m)
    return e0 / (e0 + e1)


def _split3(g):
    hi = g.astype(BF16)
    r = g - hi.astype(F32)
    mid = r.astype(BF16)
    lo = (r - mid.astype(F32)).astype(BF16)
    return hi, mid, lo


def _apply01(u01, g):
    hi, mid, lo = _split3(g)
    return _dot(u01, hi) + _dot(u01, mid) + _dot(u01, lo)


def _one_zero(mask):
    return jnp.where(mask, 1.0, 0.0).astype(BF16)


def _params(sem):
    return pltpu.CompilerParams(dimension_semantics=sem, vmem_limit_bytes=VMEM_LIMIT)


def _resident(shape, index_map):
    return pl.BlockSpec(shape, index_map, pipeline_mode=pl.Buffered(1))


def _adaln_kernel(c_ref, w_ref, b_ref, o_ref):
    c = c_ref[...]
    o_ref[...] = _dot(_silu(c).astype(BF16), w_ref[...].astype(BF16)) + b_ref[...]


def _adaln(cc, w_mod, b_mod):
    n = w_mod.shape[1]
    tn = 768
    return pl.pallas_call(
        _adaln_kernel,
        grid=(n // tn,),
        in_specs=[pl.BlockSpec((MOD_ROWS, D_MODEL), lambda j: (0, 0)),
                  pl.BlockSpec((D_MODEL, tn), lambda j: (0, j)),
                  pl.BlockSpec((1, tn), lambda j: (0, j))],
        out_specs=pl.BlockSpec((MOD_ROWS, tn), lambda j: (0, j)),
        out_shape=jax.ShapeDtypeStruct((MOD_ROWS, n), F32),
        compiler_params=_params(("arbitrary",)),
        name="adaln",
    )(cc, w_mod, b_mod.reshape(1, n))


def _ctx_kernel(ctx_ref, g1_ref, mod_ref, lbl_ref, wi_ref, wf_ref, wb_ref, sf_ref, sb_ref, *, ctx_row):
    x = ctx_ref[0]
    n = x.shape[0]
    sh = mod_ref[ctx_row:ctx_row + 1, 0:D_MODEL]
    sc = mod_ref[ctx_row:ctx_row + 1, D_MODEL:2 * D_MODEL]
    u = (_rms(x, g1_ref[...]) * (1.0 + sc) + sh).astype(BF16)
    v = _dot(u, wi_ref[...]).astype(BF16)
    zf = _dot(u, wf_ref[...])
    zb = _dot(u, wb_ref[...])
    lbf = _lower_bound(lbl_ref[0:1, :], lbl_ref[1:2, :])
    lbb = _lower_bound(lbl_ref[2:3, :], lbl_ref[3:4, :])
    ff = lbf + (1.0 - lbf) * jax.nn.sigmoid(zf)
    fb = lbb + (1.0 - lbb) * jax.nn.sigmoid(zb)
    r = lax.broadcasted_iota(I32, (n, n), 0)
    c = lax.broadcasted_iota(I32, (n, n), 1)
    ef = _apply01(_one_zero(c > r), jnp.log(ff))
    eb = _apply01(_one_zero(c < r), jnp.log(fb))
    kf = ((1.0 - ff) * jnp.exp(ef)).astype(BF16)
    kb = ((1.0 - fb) * jnp.exp(eb)).astype(BF16)
    for h in range(N_HEADS):
        hs = slice(h * HEAD_DIM, (h + 1) * HEAD_DIM)
        sf_ref[0, h] = _dot_tn(kf[:, hs], v[:, hs])
        sb_ref[0, h] = _dot_tn(kb[:, hs], v[:, hs])


def _ctx_states(ctx, norm1_g, mod, lbl, w_in_bf):
    b, lc, d = ctx.shape
    col = lambda k: pl.BlockSpec((d, HGRN_WIDTH), lambda i, k=k: (0, k))
    state = jax.ShapeDtypeStruct((b, N_HEADS, HEAD_DIM, HEAD_DIM), F32)
    sspec = pl.BlockSpec((1, N_HEADS, HEAD_DIM, HEAD_DIM), lambda i: (i, 0, 0, 0))
    return pl.pallas_call(
        functools.partial(_ctx_kernel, ctx_row=b),
        grid=(b,),
        in_specs=[pl.BlockSpec((1, lc, d), lambda i: (i, 0, 0)),
                  pl.BlockSpec((1, d), lambda i: (0, 0)),
                  pl.BlockSpec(mod.shape, lambda i: (0, 0)),
                  pl.BlockSpec(lbl.shape, lambda i: (0, 0)),
                  col(I_OFF // HGRN_WIDTH), col(FF_OFF // HGRN_WIDTH), col(FB_OFF // HGRN_WIDTH)],
        out_specs=[sspec, sspec],
        out_shape=[state, state],
        compiler_params=_params(("arbitrary",)),
        name="ctx_state",
    )(ctx, norm1_g, mod, lbl, w_in_bf, w_in_bf, w_in_bf)


def _inproj_kernel(x_ref, g1_ref, mod_ref, lbl_ref, w_ref,
                   q_ref, v_ref, kf_ref, gf_ref, kb_ref, gb_ref, og_ref, p_ref, ga_ref, gbm_ref):
    b = pl.program_id(0)
    x = x_ref[0]
    m = mod_ref[pl.ds(b, 1), :]
    sh = m[:, 0:D_MODEL]
    sc = m[:, D_MODEL:2 * D_MODEL]
    u = (_rms(x, g1_ref[...]) * (1.0 + sc) + sh).astype(BF16)

    z = _dot(u, w_ref[:, Q_OFF:I_OFF])
    q_ref[0] = _silu(z).astype(BF16)
    v_ref[0] = _dot(u, w_ref[:, I_OFF:FF_OFF]).astype(BF16)

    lbf = _lower_bound(lbl_ref[0:1, :], lbl_ref[1:2, :])
    f = lbf + (1.0 - lbf) * jax.nn.sigmoid(_dot(u, w_ref[:, FF_OFF:FB_OFF]))
    kf_ref[0] = (1.0 - f).astype(BF16)
    gf_ref[0] = jnp.log(f)
    lbb = _lower_bound(lbl_ref[2:3, :], lbl_ref[3:4, :])
    f = lbb + (1.0 - lbb) * jax.nn.sigmoid(_dot(u, w_ref[:, FB_OFF:OG_OFF]))
    kb_ref[0] = (1.0 - f).astype(BF16)
    gb_ref[0] = jnp.log(f)

    z = _dot(u, w_ref[:, OG_OFF:P_OFF])
    og_ref[0] = _silu(z).astype(BF16)
    p_ref[0] = _dot(u, w_ref[:, P_OFF:GA_OFF])
    ga_ref[0] = jax.nn.sigmoid(_dot(u, w_ref[:, GA_OFF:GB_OFF])).astype(BF16)
    gbm_ref[0] = jax.nn.sigmoid(_dot(u, w_ref[:, GB_OFF:IN_COLS])).astype(BF16)


def _in_proj(x, norm1_g, mod, lbl, w_in_bf):
    b, l, d = x.shape
    tm = TOKEN_TILE
    tok = lambda w: pl.BlockSpec((1, tm, w), lambda i, j: (i, j, 0))
    wide = lambda dt, w=HGRN_WIDTH: jax.ShapeDtypeStruct((b, l, w), dt)
    return pl.pallas_call(
        _inproj_kernel,
        grid=(b, l // tm),
        in_specs=[tok(d),
                  pl.BlockSpec((1, d), lambda i, j: (0, 0)),
                  pl.BlockSpec(mod.shape, lambda i, j: (0, 0)),
                  pl.BlockSpec(lbl.shape, lambda i, j: (0, 0)),
                  _resident((d, IN_COLS), lambda i, j: (0, 0))],
        out_specs=[tok(HGRN_WIDTH)] * 7 + [tok(POOL_WIDTH), tok(d), tok(d)],
        out_shape=[wide(BF16), wide(BF16), wide(BF16), wide(F32), wide(BF16), wide(F32), wide(BF16),
                   wide(F32, POOL_WIDTH), wide(BF16, d), wide(BF16, d)],
        compiler_params=_params(("arbitrary", "arbitrary")),
        name="in_proj",
    )(x, norm1_g, mod, lbl, w_in_bf)


def _hgrn_direction(q_ref, v_ref, k_ref, g_ref, o_ref, s_ref, oi_ref, b_ref, kk_ref, vv_ref, reverse):
    t = HGRN_CHUNK
    nb = t // HGRN_BLOCK
    q = q_ref[0].astype(F32)
    k = k_ref[0].astype(F32)
    v = v_ref[0]
    row = lax.broadcasted_iota(I32, (t, t), 0)
    col = lax.broadcasted_iota(I32, (t, t), 1)
    seen = (col >= row) if reverse else (col <= row)
    binc = _apply01(_one_zero(seen), g_ref[0])

    def edge(j):
        if reverse:
            r = (j + 1) * HGRN_BLOCK
            return None if j == nb - 1 else binc[r:r + 1, :]
        r = j * HGRN_BLOCK
        return None if j == 0 else binc[r - 1:r, :]

    zeros = lambda n: jnp.zeros((n, HGRN_WIDTH), BF16)
    q_slabs, k_slabs = [], []
    worst = None
    for j in range(nb):
        lo, hi = j * HGRN_BLOCK, (j + 1) * HGRN_BLOCK
        ref = edge(j)
        rows = slice(0, hi) if reverse else slice(lo, t)
        arg = binc[rows] if ref is None else binc[rows] - ref
        qs = (q[rows] * jnp.exp(arg)).astype(BF16)
        q_slabs.append(jnp.concatenate([qs, zeros(t - hi)] if reverse else [zeros(lo), qs], axis=0)
                       if (t - hi if reverse else lo) else qs)
        blk = binc[lo:hi]
        karg = -blk if ref is None else ref - blk
        ks = (k[lo:hi] * jnp.exp(karg)).astype(BF16)
        parts = ([zeros(lo)] if lo else []) + [ks] + ([zeros(t - hi)] if t - hi else [])
        k_slabs.append(jnp.concatenate(parts, axis=0) if len(parts) > 1 else ks)
        far = binc[lo:lo + 1] if reverse else binc[hi - 1:hi]
        dec = far if ref is None else far - ref
        worst = dec if worst is None else jnp.minimum(worst, dec)
    q_full = q_slabs[nb - 1] if reverse else q_slabs[0]
    b_end = binc[0:1] if reverse else binc[t - 1:t]
    k_state = (k * jnp.exp(b_end - binc)).astype(BF16)
    safe = jnp.min(worst) >= -HGRN_SAFE_DECAY

    @pl.when(safe)
    def _():
        for h in range(N_HEADS):
            hs = slice(h * HEAD_DIM, (h + 1) * HEAD_DIM)
            qc = jnp.concatenate([s[:, hs] for s in q_slabs], axis=1)
            kc = jnp.concatenate([s[:, hs] for s in k_slabs], axis=1)
            p = jnp.where(seen, _dot_nt(qc, kc), 0.0).astype(BF16)
            oi_ref[:, hs] = _dot(p, v[:, hs])

    @pl.when(jnp.logical_not(safe))
    def _():
        b_ref[...] = binc
        kk_ref[...] = k
        vv_ref[...] = v.astype(F32)
        oi_ref[...] = jnp.zeros((t, HGRN_WIDTH), F32)
        trow = lax.broadcasted_iota(I32, (t, HGRN_WIDTH), 0)

        def body(s, carry):
            reads = (trow <= s) if reverse else (trow >= s)
            w = jnp.exp(jnp.where(reads, binc - b_ref[pl.ds(s, 1), :], 0.0))
            prod = jnp.where(reads, q * kk_ref[pl.ds(s, 1), :] * w, 0.0)
            vs = vv_ref[pl.ds(s, 1), :]
            for h in range(N_HEADS):
                hs = slice(h * HEAD_DIM, (h + 1) * HEAD_DIM)
                oi_ref[:, hs] += jnp.sum(prod[:, hs], axis=1, keepdims=True) * vs[:, hs]
            return carry

        lax.fori_loop(0, t, body, 0)

    e_end = jnp.exp(b_end)
    e_rows = jnp.concatenate([e_end[:, h * HEAD_DIM:(h + 1) * HEAD_DIM] for h in range(N_HEADS)]
                             + [jnp.zeros((HEAD_DIM - N_HEADS, HEAD_DIM), F32)], axis=0)
    e_cols = e_rows.T
    for h in range(N_HEADS):
        hs = slice(h * HEAD_DIM, (h + 1) * HEAD_DIM)
        s_old = s_ref[h]
        o_ref[0, :, hs] = oi_ref[:, hs] + _dot(q_full[:, hs], s_old.astype(BF16))
        s_ref[h] = s_old * e_cols[:, h:h + 1] + _dot_tn(k_state[:, hs], v[:, hs])


def _hgrn_kernel(qf_ref, vf_ref, kf_ref, gf_ref, qb_ref, vb_ref, kb_ref, gb_ref, s0f_ref, s0b_ref,
                 of_ref, ob_ref, sf_ref, sb_ref, oi_ref, b_ref, kk_ref, vv_ref):
    @pl.when(pl.program_id(1) == 0)
    def _():
        sf_ref[...] = s0f_ref[0]
        sb_ref[...] = s0b_ref[0]

    _hgrn_direction(qf_ref, vf_ref, kf_ref, gf_ref, of_ref, sf_ref, oi_ref, b_ref, kk_ref, vv_ref, False)
    _hgrn_direction(qb_ref, vb_ref, kb_ref, gb_ref, ob_ref, sb_ref, oi_ref, b_ref, kk_ref, vv_ref, True)


def _hgrn(q, v, kf, gf, kb, gb, s0f, s0b):
    b, l, w = q.shape
    t = HGRN_CHUNK
    n = l // t
    fwd = pl.BlockSpec((1, t, w), lambda i, j: (i, j, 0))
    bwd = pl.BlockSpec((1, t, w), lambda i, j: (i, n - 1 - j, 0))
    sspec = pl.BlockSpec((1, N_HEADS, HEAD_DIM, HEAD_DIM), lambda i, j: (i, 0, 0, 0))
    out = jax.ShapeDtypeStruct((b, l, w), F32)
    return pl.pallas_call(
        _hgrn_kernel,
        grid=(b, n),
        in_specs=[fwd, fwd, fwd, fwd, bwd, bwd, bwd, bwd, sspec, sspec],
        out_specs=[fwd, bwd],
        out_shape=[out, out],
        scratch_shapes=[pltpu.VMEM((N_HEADS, HEAD_DIM, HEAD_DIM), F32),
                        pltpu.VMEM((N_HEADS, HEAD_DIM, HEAD_DIM), F32),
                        pltpu.VMEM((t, w), F32), pltpu.VMEM((t, w), F32), pltpu.VMEM((t, w), F32),
                        pltpu.VMEM((t, w), F32)],
        compiler_params=_params(("arbitrary", "arbitrary")),
        name="hgrn",
    )(q, v, kf, gf, q, v, kb, gb, s0f, s0b)


def _pool_col_matrices():
    mats = np.zeros((len(POOL_WINDOWS), POOL_TILE, POOL_TILE), np.float32)
    for j, w in enumerate(POOL_WINDOWS):
        for t in range(POOL_TILE):
            r, c = divmod(t, GRID_W)
            lo, hi = max(c - w // 2, 0), min(c + w // 2 - 1, GRID_W - 1)
            mats[j, t, r * GRID_W + lo:r * GRID_W + hi + 1] = 1.0
    return jnp.asarray(mats, BF16)


def _pool_kernel(p_ref, a_ref, wp_ref, ps_ref, o_ref, pad_ref, *, rows):
    j = pl.program_id(1)
    half = jnp.left_shift(1, j)
    halo = POOL_HALO * GRID_W
    l = rows * GRID_W
    pad_ref[0:halo, :] = jnp.zeros((halo, POOL_GROUP), F32)
    pad_ref[halo + l:2 * halo + l, :] = jnp.zeros((halo, POOL_GROUP), F32)
    pad_ref[halo:halo + l, :] = p_ref[0]
    a = a_ref[0]
    wp = wp_ref[0]
    scale = ps_ref[0]

    def step(i, carry):
        t0 = pl.multiple_of(i * POOL_TILE, POOL_TILE)

        def add_row(d, acc):
            off = pl.multiple_of(t0 + (POOL_HALO + d - half) * GRID_W, GRID_W)
            return acc + pad_ref[pl.ds(off, POOL_TILE), :]

        rsum = lax.fori_loop(0, 2 * half, add_row, jnp.zeros((POOL_TILE, POOL_GROUP), F32))
        total = _apply01(a, rsum)
        tok = t0 + lax.broadcasted_iota(I32, (POOL_TILE, POOL_GROUP), 0)
        r = jnp.right_shift(tok, 6)
        c = jnp.bitwise_and(tok, GRID_W - 1)
        nr = jnp.minimum(r + half - 1, rows - 1) + 1 - jnp.maximum(r - half, 0)
        nc = jnp.minimum(c + half - 1, GRID_W - 1) + 1 - jnp.maximum(c - half, 0)
        x = pad_ref[pl.ds(pl.multiple_of(t0 + halo, GRID_W), POOL_TILE), :]
        resid = total / (nr * nc).astype(F32) - x
        o_ref[0, pl.ds(t0, POOL_TILE), :] = (_dot(resid.astype(BF16), wp) * scale).astype(BF16)
        return carry

    lax.fori_loop(0, l // POOL_TILE, step, 0)


def _pool(p, w_pool_bf, pool_scale):
    b, l, _ = p.shape
    rows = l // GRID_W
    ng = len(POOL_WINDOWS)
    grp = pl.BlockSpec((1, l, POOL_GROUP), lambda i, j: (i, 0, j))
    return pl.pallas_call(
        functools.partial(_pool_kernel, rows=rows),
        grid=(b, ng),
        in_specs=[grp,
                  pl.BlockSpec((1, POOL_TILE, POOL_TILE), lambda i, j: (j, 0, 0)),
                  pl.BlockSpec((1, POOL_GROUP, POOL_GROUP), lambda i, j: (j, 0, 0)),
                  pl.BlockSpec((1, 1, POOL_GROUP), lambda i, j: (j, 0, 0))],
        out_specs=grp,
        out_shape=jax.ShapeDtypeStruct((b, l, POOL_WIDTH), BF16),
        scratch_shapes=[pltpu.VMEM(((rows + 2 * POOL_HALO) * GRID_W, POOL_GROUP), F32)],
        compiler_params=_params(("arbitrary", "arbitrary")),
        name="pool",
    )(p, _pool_col_matrices(), w_pool_bf, pool_scale.reshape(ng, 1, POOL_GROUP))


def _merge_kernel(x_ref, of_ref, ob_ref, og_ref, pl_ref, ga_ref, gb_ref, mod_ref, hg_ref, n2_ref,
                  wa_ref, wb_ref, wo_ref, wr_ref, nh_ref, vx_ref, pr_ref):
    b = pl.program_id(0)
    m = mod_ref[pl.ds(b, 1), :]
    g1 = m[:, 2 * D_MODEL:3 * D_MODEL]
    sh2 = m[:, 3 * D_MODEL:4 * D_MODEL]
    sc2 = m[:, 4 * D_MODEL:5 * D_MODEL]
    o = of_ref[0] + ob_ref[0]
    hg = hg_ref[...]
    og = og_ref[0]
    gated = []
    for h in range(N_HEADS):
        hs = slice(h * HEAD_DIM, (h + 1) * HEAD_DIM)
        gated.append((_rms(o[:, hs], hg) * og[:, hs].astype(F32)).astype(BF16))
    ya = _dot(jnp.concatenate(gated, axis=1), wa_ref[...])
    yb = _dot(pl_ref[0], wb_ref[...])
    y = ga_ref[0].astype(F32) * ya + gb_ref[0].astype(F32) * yb
    nh = x_ref[0] + g1 * _dot(y.astype(BF16), wo_ref[...])
    nh_ref[0] = nh
    vx = (_rms(nh, n2_ref[...]) * (1.0 + sc2) + sh2).astype(BF16)
    vx_ref[0] = vx
    logits = _dot_nt(wr_ref[...], vx)
    ex = jnp.exp(logits - jnp.max(logits, axis=0, keepdims=True))
    pr_ref[0] = ex / jnp.sum(ex, axis=0, keepdims=True)


def _merge(x, o_f, o_b, og, pooled, ga, gb, mod, hgrn_g, norm2_g, w_a_bf, w_b_bf, w_out_bf, w_router_t):
    b, l, d = x.shape
    tm = TOKEN_TILE
    tok = lambda w: pl.BlockSpec((1, tm, w), lambda i, j: (i, j, 0))
    const = lambda a: pl.BlockSpec(a.shape, lambda i, j: (0,) * a.ndim)
    return pl.pallas_call(
        _merge_kernel,
        grid=(b, l // tm),
        in_specs=[tok(d), tok(d), tok(d), tok(d), tok(POOL_WIDTH), tok(d), tok(d),
                  const(mod), const(hgrn_g), const(norm2_g),
                  _resident(w_a_bf.shape, lambda i, j: (0, 0)),
                  _resident(w_b_bf.shape, lambda i, j: (0, 0)),
                  _resident(w_out_bf.shape, lambda i, j: (0, 0)),
                  const(w_router_t)],
        out_specs=[tok(d), tok(d), pl.BlockSpec((1, N_EXPERTS, tm), lambda i, j: (i, 0, j))],
        out_shape=[jax.ShapeDtypeStruct((b, l, d), F32),
                   jax.ShapeDtypeStruct((b, l, d), BF16),
                   jax.ShapeDtypeStruct((b, N_EXPERTS, l), F32)],
        compiler_params=_params(("arbitrary", "arbitrary")),
        name="merge",
    )(x, o_f, o_b, og, pooled, ga, gb, mod, hgrn_g, norm2_g, w_a_bf, w_b_bf, w_out_bf, w_router_t)


def _route_kernel(p_ref, code_ref, start_ref, *, cap):
    l = p_ref.shape[2]
    tm = TOKEN_TILE
    bits = lax.bitcast_convert_type(p_ref[0], I32)
    capf = jnp.float32(cap)

    def count_ge(cand):
        return jnp.sum(jnp.where(bits >= cand, 1.0, 0.0), axis=1, keepdims=True)

    def search(i, prefix):
        cand = jnp.bitwise_or(prefix, jnp.left_shift(1, 30 - i))
        return jnp.where(count_ge(cand) >= capf, cand, prefix)

    tau = lax.fori_loop(0, 31, search, jnp.zeros((N_EXPERTS, 1), I32))
    need = capf - jnp.sum(jnp.where(bits > tau, 1.0, 0.0), axis=1, keepdims=True)
    before = _one_zero(lax.broadcasted_iota(I32, (tm, tm), 0) < lax.broadcasted_iota(I32, (tm, tm), 1))
    lane = lax.broadcasted_iota(I32, (N_EXPERTS, 128), 1)

    def tile(k, carry):
        n_gt, n_eq, starts = carry
        sl = pl.ds(pl.multiple_of(k * tm, tm), tm)
        bk = lax.bitcast_convert_type(p_ref[0, :, sl], I32)
        gt = bk > tau
        eq = bk == tau
        gt01 = _one_zero(gt)
        eq01 = _one_zero(eq)
        eq_before = n_eq + _dot(eq01, before)
        chosen = jnp.logical_or(gt, jnp.logical_and(eq, eq_before < need))
        rank = n_gt + _dot(gt01, before) + jnp.minimum(eq_before, need)
        code_ref[0, :, sl] = jnp.where(chosen, rank, -1.0).astype(I32)
        starts = jnp.where(lane == k, (n_gt + jnp.minimum(n_eq, need)).astype(I32), starts)
        n_gt = n_gt + jnp.sum(gt01.astype(F32), axis=1, keepdims=True)
        n_eq = n_eq + jnp.sum(eq01.astype(F32), axis=1, keepdims=True)
        return n_gt, n_eq, starts

    zero = jnp.zeros((N_EXPERTS, 1), F32)
    _, _, starts = lax.fori_loop(0, l // tm, tile, (zero, zero, jnp.zeros((N_EXPERTS, 128), I32)))
    start_ref[0] = jnp.where(lane == l // tm, cap, starts)


def _route(probs_t, cap):
    b, e, l = probs_t.shape
    assert l // TOKEN_TILE < 128
    return pl.pallas_call(
        functools.partial(_route_kernel, cap=cap),
        grid=(b,),
        in_specs=[pl.BlockSpec((1, e, l), lambda i: (i, 0, 0))],
        out_specs=[pl.BlockSpec((1, e, l), lambda i: (i, 0, 0)),
                   pl.BlockSpec((1, e, 128), lambda i: (i, 0, 0))],
        out_shape=[jax.ShapeDtypeStruct((b, e, l), I32), jax.ShapeDtypeStruct((b, e, 128), I32)],
        compiler_params=_params(("arbitrary",)),
        name="route",
    )(probs_t)


def _moe_kernel(start_ref, vx_ref, code_ref, p_ref, wg_ref, wu_ref, wd_ref, y_ref, xg_ref, ps_ref,
                *, cap, n_tiles):
    b = pl.program_id(0)
    e = pl.program_id(1)
    tm = TOKEN_TILE
    w = SLOT_WINDOW
    base_idx = (b * N_EXPERTS + e) * 128
    xg_ref[...] = jnp.zeros(xg_ref.shape, F32)
    ps_ref[...] = jnp.zeros(ps_ref.shape, F32)
    slot = lax.broadcasted_iota(I32, (w, tm), 0)

    def tile(k, carry):
        st = start_ref[base_idx + k]
        cnt = start_ref[base_idx + k + 1] - st
        st_al = jnp.left_shift(jnp.right_shift(st, 3), 3)
        nch = jnp.where(cnt > 0, jnp.right_shift(st - st_al + cnt + w - 1, SLOT_SHIFT), 0)
        tsl = pl.ds(pl.multiple_of(k * tm, tm), tm)
        codes = code_ref[0, pl.ds(e, 1), tsl]
        probs = p_ref[0, pl.ds(e, 1), tsl]

        def chunk(c, carry2):
            base = pl.multiple_of(st_al + c * w, 8)
            hit = codes == (slot + base)
            rows = pl.ds(base, w)
            xg_ref[rows, :] += _dot(_one_zero(hit), vx_ref[0, tsl, :])
            ps_ref[rows, :] += jnp.sum(jnp.where(hit, probs, 0.0), axis=1, keepdims=True)
            return carry2

        lax.fori_loop(0, nch, chunk, 0)
        return carry

    lax.fori_loop(0, n_tiles, tile, 0)
    fr = min(FFN_ROWS, cap)
    for r0 in range(0, cap, fr):
        xg = xg_ref[r0:r0 + fr, :].astype(BF16)
        hid = (_silu(_dot(xg, wg_ref[0])) * _dot(xg, wu_ref[0])).astype(BF16)
        y_ref[0, 0, r0:r0 + fr, :] = (_dot(hid, wd_ref[0]) * ps_ref[r0:r0 + fr, 0:1]).astype(BF16)


def _moe(starts_flat, vx, code, probs_t, wg_bf, wu_bf, wd_bf, cap):
    b, l, d = vx.shape
    n_tiles = l // TOKEN_TILE
    grid_spec = pltpu.PrefetchScalarGridSpec(
        num_scalar_prefetch=1,
        grid=(b, N_EXPERTS),
        in_specs=[_resident((1, l, d), lambda i, e, s: (i, 0, 0)),
                  pl.BlockSpec((1, N_EXPERTS, l), lambda i, e, s: (i, 0, 0)),
                  pl.BlockSpec((1, N_EXPERTS, l), lambda i, e, s: (i, 0, 0)),
                  pl.BlockSpec((1, d, EXPERT_FF), lambda i, e, s: (e, 0, 0)),
                  pl.BlockSpec((1, d, EXPERT_FF), lambda i, e, s: (e, 0, 0)),
                  pl.BlockSpec((1, EXPERT_FF, d), lambda i, e, s: (e, 0, 0))],
        out_specs=pl.BlockSpec((1, 1, cap, d), lambda i, e, s: (i, e, 0, 0)),
        scratch_shapes=[pltpu.VMEM((cap + SLOT_WINDOW, d), F32),
                        pltpu.VMEM((cap + SLOT_WINDOW, 128), F32)],
    )
    return pl.pallas_call(
        functools.partial(_moe_kernel, cap=cap, n_tiles=n_tiles),
        grid_spec=grid_spec,
        out_shape=jax.ShapeDtypeStruct((b, N_EXPERTS, cap, d), BF16),
        compiler_params=_params(("arbitrary", "arbitrary")),
        name="moe",
    )(starts_flat, vx, code, probs_t, wg_bf, wu_bf, wd_bf)


def _combine_kernel(start_ref, nh_ref, code_ref, mod_ref, fg_ref, y_hbm, o_ref, ybuf, sem,
                    *, cap, n_tiles, n_steps):
    b = pl.program_id(0)
    k = pl.program_id(1)
    step = b * n_tiles + k
    tm = TOKEN_TILE
    w = SLOT_WINDOW

    def window(bb, kk, e, c):
        st = start_ref[(bb * N_EXPERTS + e) * 128 + kk]
        lo = jnp.left_shift(jnp.right_shift(st, 3), 3) + c * w
        return lo, pl.multiple_of(jnp.minimum(lo, cap - w), 8)

    def copies(bb, kk, buf):
        out = []
        for e in range(N_EXPERTS):
            _, src = window(bb, kk, e, 0)
            out.append(pltpu.make_async_copy(y_hbm.at[bb, e, pl.ds(src, w), :], ybuf.at[buf, e], sem.at[buf, e]))
        return out

    @pl.when(step == 0)
    def _():
        for cp in copies(b, k, 0):
            cp.start()

    cur = lax.rem(step, 2)

    @pl.when(step + 1 < n_steps)
    def _():
        nk = jnp.where(k + 1 == n_tiles, 0, k + 1)
        nb = jnp.where(k + 1 == n_tiles, b + 1, b)
        for cp in copies(nb, nk, 1 - cur):
            cp.start()

    for cp in copies(b, k, cur):
        cp.wait()

    slot = lax.broadcasted_iota(I32, (w, tm), 0)

    def scatter(c, buf_rows):
        hits = []
        for e in range(N_EXPERTS):
            lo, src = window(b, k, e, c)
            codes = code_ref[0, e:e + 1, :]
            hit = jnp.logical_and(codes == (slot + src), jnp.logical_and(codes >= lo, codes < lo + w))
            hits.append(_one_zero(hit))
        return _dot_tn(jnp.concatenate(hits, axis=0), buf_rows)

    acc = scatter(0, ybuf[cur].reshape(N_EXPERTS * w, D_MODEL))

    most = jnp.int32(0)
    for e in range(N_EXPERTS):
        i0 = (b * N_EXPERTS + e) * 128 + k
        st = start_ref[i0]
        span = st - jnp.left_shift(jnp.right_shift(st, 3), 3) + start_ref[i0 + 1] - st
        most = jnp.maximum(most, jnp.right_shift(span + w - 1, SLOT_SHIFT))

    def extra(c, acc):
        cps = []
        for e in range(N_EXPERTS):
            _, src = window(b, k, e, c)
            cps.append(pltpu.make_async_copy(y_hbm.at[b, e, pl.ds(src, w), :], ybuf.at[cur, e], sem.at[cur, e]))
        for cp in cps:
            cp.start()
        for cp in cps:
            cp.wait()
        return acc + scatter(c, ybuf[cur].reshape(N_EXPERTS * w, D_MODEL))

    acc = lax.fori_loop(1, most, extra, acc)
    g2 = mod_ref[pl.ds(b, 1), 5 * D_MODEL:6 * D_MODEL]
    o_ref[0] = _rms(nh_ref[0] + g2 * acc, fg_ref[...])


def _combine(starts_flat, new_hx, code, mod, final_g, y, cap):
    b, l, d = new_hx.shape
    tm = TOKEN_TILE
    n_tiles = l // tm
    grid_spec = pltpu.PrefetchScalarGridSpec(
        num_scalar_prefetch=1,
        grid=(b, n_tiles),
        in_specs=[pl.BlockSpec((1, tm, d), lambda i, j, s: (i, j, 0)),
                  pl.BlockSpec((1, N_EXPERTS, tm), lambda i, j, s: (i, 0, j)),
                  pl.BlockSpec(mod.shape, lambda i, j, s: (0, 0)),
                  pl.BlockSpec((1, d), lambda i, j, s: (0, 0)),
                  pl.BlockSpec(memory_space=pl.ANY)],
        out_specs=pl.BlockSpec((1, tm, d), lambda i, j, s: (i, j, 0)),
        scratch_shapes=[pltpu.VMEM((2, N_EXPERTS, SLOT_WINDOW, d), BF16),
                        pltpu.SemaphoreType.DMA((2, N_EXPERTS))],
    )
    return pl.pallas_call(
        functools.partial(_combine_kernel, cap=cap, n_tiles=n_tiles, n_steps=b * n_tiles),
        grid_spec=grid_spec,
        out_shape=jax.ShapeDtypeStruct((b, l, d), F32),
        compiler_params=_params(("arbitrary", "arbitrary")),
        name="combine",
    )(starts_flat, new_hx, code, mod, final_g, y)


def _layer(x, c, ctx, c_ctx, lb_logits, w_mod, b_mod, norm1_g, w_in, hgrn_norm_g, w_a, w_pool, pool_scale,
           w_b, w_out, norm2_g, w_router, w_e_gate, w_e_up, w_e_down, final_g):
    b, l, d = x.shape
    cap = CAPACITY_FACTOR * l // N_EXPERTS
    assert b < MOD_ROWS and l % HGRN_CHUNK == 0 and l % TOKEN_TILE == 0 and cap % 8 == 0 and cap >= SLOT_WINDOW
    cc = jnp.zeros((MOD_ROWS, d), F32).at[:b].set(c).at[b].set(c_ctx)
    assert lb_logits.shape == (2, 2, HGRN_WIDTH)
    lbl = lb_logits.reshape(4, HGRN_WIDTH)
    row = lambda a: a.reshape(1, -1)
    w_in_bf = w_in.astype(BF16)

    mod = _adaln(cc, w_mod, b_mod)
    s0f, s0b = _ctx_states(ctx, row(norm1_g), mod, lbl, w_in_bf)
    q, v, kf, gf, kb, gb, og, p, ga, gbm = _in_proj(x, row(norm1_g), mod, lbl, w_in_bf)
    o_f, o_b = _hgrn(q, v, kf, gf, kb, gb, s0f, s0b)
    pooled = _pool(p, w_pool.astype(BF16), pool_scale)
    new_hx, vx, probs_t = _merge(x, o_f, o_b, og, pooled, ga, gbm, mod, row(hgrn_norm_g), row(norm2_g),
                                 w_a.astype(BF16), w_b.astype(BF16), w_out.astype(BF16),
                                 w_router.T.astype(BF16))
    code, starts = _route(probs_t, cap)
    starts_flat = starts.reshape(-1)
    y = _moe(starts_flat, vx, code, probs_t, w_e_gate.astype(BF16), w_e_up.astype(BF16),
             w_e_down.astype(BF16), cap)
    return _combine(starts_flat, new_hx, code, mod, row(final_g), y, cap)


def kernel(x, c, ctx, c_ctx, lb_logits, w_mod, b_mod, norm1_g, w_in, hgrn_norm_g, w_a, w_pool, pool_scale,
           w_b, w_out, norm2_g, w_router, w_e_gate, w_e_up, w_e_down, final_g):
    assert w_mod.shape[0] == 1, "single-layer trunk: the context stream only seeds the latent recurrence"
    return _layer(x, c, ctx, c_ctx, lb_logits, w_mod[0], b_mod[0], norm1_g[0], w_in[0], hgrn_norm_g[0], w_a[0],
                  w_pool[0], pool_scale[0], w_b[0], w_out[0], norm2_g[0], w_router[0], w_e_gate[0],
                  w_e_up[0], w_e_down[0], final_g)
```

```python
import functools

import numpy as np
import jax
import jax.numpy as jnp
from jax import lax
from jax.experimental import pallas as pl
from jax.experimental.pallas import tpu as pltpu

F32 = jnp.float32
BF16 = jnp.bfloat16
I32 = jnp.int32

D_MODEL = 1024
N_HEADS = 8
HEAD_DIM = 128
HGRN_WIDTH = N_HEADS * HEAD_DIM
GRID_W = 64
POOL_WINDOWS = (2, 4, 8, 16)
POOL_GROUP = 128
POOL_WIDTH = POOL_GROUP * len(POOL_WINDOWS)
N_EXPERTS = 16
CAPACITY_FACTOR = 2
EXPERT_FF = 1024
EPS = 1e-6

Q_OFF = 0
I_OFF = Q_OFF + HGRN_WIDTH
FF_OFF = I_OFF + HGRN_WIDTH
FB_OFF = FF_OFF + HGRN_WIDTH
OG_OFF = FB_OFF + HGRN_WIDTH
P_OFF = OG_OFF + HGRN_WIDTH
GA_OFF = P_OFF + POOL_WIDTH
GB_OFF = GA_OFF + D_MODEL
IN_COLS = GB_OFF + D_MODEL

MOD_ROWS = 16
PROJ_TILE = 512
TOKEN_TILE = 256
GATHER_UNROLL = 4
HGRN_CHUNK = 128
HGRN_BLOCK = 32
HGRN_SAFE_DECAY = 80.0
POOL_HALO = 16
POOL_TILE = 256
POOL_UNROLL = 4
SLOT_SHIFT = 6
SLOT_WINDOW = 1 << SLOT_SHIFT
FFN_ROWS = 256
VMEM_LIMIT = 56 * 1024 * 1024


def _dot(a, b):
    return jnp.dot(a, b, preferred_element_type=F32)


def _dot_nt(a, b):
    return lax.dot_general(a, b, (((1,), (1,)), ((), ())), preferred_element_type=F32)


def _dot_tn(a, b):
    return lax.dot_general(a, b, (((0,), (0,)), ((), ())), preferred_element_type=F32)


def _rms(x, g):
    ms = jnp.mean(x * x, axis=-1, keepdims=True)
    return x * lax.rsqrt(ms + EPS) * g


def _silu(z):
    return z * jax.nn.sigmoid(z)


def _lower_bound(a0, a1):
    m = jnp.maximum(a0, a1)
    e0 = jnp.exp(a0 - m)
    e1 = jnp.exp(a1 - m)
    return e0 / (e0 + e1)


def _split3(g):
    hi = g.astype(BF16)
    r = g - hi.astype(F32)
    mid = r.astype(BF16)
    lo = (r - mid.astype(F32)).astype(BF16)
    return hi, mid, lo


def _apply01(u01, g):
    hi, mid, lo = _split3(g)
    return _dot(u01, hi) + _dot(u01, mid) + _dot(u01, lo)


def _one_zero(mask):
    return jnp.where(mask, 1.0, 0.0).astype(BF16)


def _params(sem):
    return pltpu.CompilerParams(dimension_semantics=sem, vmem_limit_bytes=VMEM_LIMIT)


def _resident(shape, index_map):
    return pl.BlockSpec(shape, index_map, pipeline_mode=pl.Buffered(1))


def _adaln_kernel(c_ref, w_ref, b_ref, o_ref):
    c = c_ref[...]
    o_ref[...] = _dot(_silu(c).astype(BF16), w_ref[...].astype(BF16)) + b_ref[...]


def _adaln(cc, w_mod, b_mod):
    n = w_mod.shape[1]
    tn = 768
    return pl.pallas_call(
        _adaln_kernel,
        grid=(n // tn,),
        in_specs=[pl.BlockSpec((MOD_ROWS, D_MODEL), lambda j: (0, 0)),
                  pl.BlockSpec((D_MODEL, tn), lambda j: (0, j)),
                  pl.BlockSpec((1, tn), lambda j: (0, j))],
        out_specs=pl.BlockSpec((MOD_ROWS, tn), lambda j: (0, j)),
        out_shape=jax.ShapeDtypeStruct((MOD_ROWS, n), F32),
        compiler_params=_params(("arbitrary",)),
        name="adaln",
    )(cc, w_mod, b_mod.reshape(1, n))


def _ctx_kernel(ctx_ref, g1_ref, mod_ref, lbl_ref, wi_ref, wf_ref, wb_ref, sf_ref, sb_ref, *, ctx_row):
    x = ctx_ref[0]
    n = x.shape[0]
    sh = mod_ref[ctx_row:ctx_row + 1, 0:D_MODEL]
    sc = mod_ref[ctx_row:ctx_row + 1, D_MODEL:2 * D_MODEL]
    u = (_rms(x, g1_ref[...]) * (1.0 + sc) + sh).astype(BF16)
    v = _dot(u, wi_ref[...]).astype(BF16)
    zf = _dot(u, wf_ref[...])
    zb = _dot(u, wb_ref[...])
    lbf = _lower_bound(lbl_ref[0:1, :], lbl_ref[1:2, :])
    lbb = _lower_bound(lbl_ref[2:3, :], lbl_ref[3:4, :])
    ff = lbf + (1.0 - lbf) * jax.nn.sigmoid(zf)
    fb = lbb + (1.0 - lbb) * jax.nn.sigmoid(zb)
    r = lax.broadcasted_iota(I32, (n, n), 0)
    c = lax.broadcasted_iota(I32, (n, n), 1)
    ef = _apply01(_one_zero(c > r), jnp.log(ff))
    eb = _apply01(_one_zero(c < r), jnp.log(fb))
    kf = ((1.0 - ff) * jnp.exp(ef)).astype(BF16)
    kb = ((1.0 - fb) * jnp.exp(eb)).astype(BF16)
    for h in range(N_HEADS):
        hs = slice(h * HEAD_DIM, (h + 1) * HEAD_DIM)
        sf_ref[0, h] = _dot_tn(kf[:, hs], v[:, hs])
        sb_ref[0, h] = _dot_tn(kb[:, hs], v[:, hs])


def _ctx_states(ctx, norm1_g, mod, lbl, w_in_bf):
    b, lc, d = ctx.shape
    col = lambda k: pl.BlockSpec((d, HGRN_WIDTH), lambda i, k=k: (0, k))
    state = jax.ShapeDtypeStruct((b, N_HEADS, HEAD_DIM, HEAD_DIM), F32)
    sspec = pl.BlockSpec((1, N_HEADS, HEAD_DIM, HEAD_DIM), lambda i: (i, 0, 0, 0))
    return pl.pallas_call(
        functools.partial(_ctx_kernel, ctx_row=b),
        grid=(b,),
        in_specs=[pl.BlockSpec((1, lc, d), lambda i: (i, 0, 0)),
                  pl.BlockSpec((1, d), lambda i: (0, 0)),
                  pl.BlockSpec(mod.shape, lambda i: (0, 0)),
                  pl.BlockSpec(lbl.shape, lambda i: (0, 0)),
                  col(I_OFF // HGRN_WIDTH), col(FF_OFF // HGRN_WIDTH), col(FB_OFF // HGRN_WIDTH)],
        out_specs=[sspec, sspec],
        out_shape=[state, state],
        compiler_params=_params(("arbitrary",)),
        name="ctx_state",
    )(ctx, norm1_g, mod, lbl, w_in_bf, w_in_bf, w_in_bf)


def _inproj_kernel(x_ref, g1_ref, mod_ref, lbl_ref, w_ref,
                   q_ref, v_ref, kf_ref, gf_ref, kb_ref, gb_ref, og_ref, p_ref, ga_ref, gbm_ref):
    b = pl.program_id(0)
    x = x_ref[0]
    m = mod_ref[pl.ds(b, 1), :]
    sh = m[:, 0:D_MODEL]
    sc = m[:, D_MODEL:2 * D_MODEL]
    u = (_rms(x, g1_ref[...]) * (1.0 + sc) + sh).astype(BF16)

    z = _dot(u, w_ref[:, Q_OFF:I_OFF])
    q_ref[0] = _silu(z).astype(BF16)
    v_ref[0] = _dot(u, w_ref[:, I_OFF:FF_OFF]).astype(BF16)

    lbf = _lower_bound(lbl_ref[0:1, :], lbl_ref[1:2, :])
    f = lbf + (1.0 - lbf) * jax.nn.sigmoid(_dot(u, w_ref[:, FF_OFF:FB_OFF]))
    kf_ref[0] = (1.0 - f).astype(BF16)
    gf_ref[0] = jnp.log(f)
    lbb = _lower_bound(lbl_ref[2:3, :], lbl_ref[3:4, :])
    f = lbb + (1.0 - lbb) * jax.nn.sigmoid(_dot(u, w_ref[:, FB_OFF:OG_OFF]))
    kb_ref[0] = (1.0 - f).astype(BF16)
    gb_ref[0] = jnp.log(f)

    z = _dot(u, w_ref[:, OG_OFF:P_OFF])
    og_ref[0] = _silu(z).astype(BF16)
    p_ref[0] = _dot(u, w_ref[:, P_OFF:GA_OFF])
    ga_ref[0] = jax.nn.sigmoid(_dot(u, w_ref[:, GA_OFF:GB_OFF])).astype(BF16)
    gbm_ref[0] = jax.nn.sigmoid(_dot(u, w_ref[:, GB_OFF:IN_COLS])).astype(BF16)


def _in_proj(x, norm1_g, mod, lbl, w_in_bf):
    b, l, d = x.shape
    tm = TOKEN_TILE
    tok = lambda w: pl.BlockSpec((1, tm, w), lambda i, j: (i, j, 0))
    wide = lambda dt, w=HGRN_WIDTH: jax.ShapeDtypeStruct((b, l, w), dt)
    return pl.pallas_call(
        _inproj_kernel,
        grid=(b, l // tm),
        in_specs=[tok(d),
                  pl.BlockSpec((1, d), lambda i, j: (0, 0)),
                  pl.BlockSpec(mod.shape, lambda i, j: (0, 0)),
                  pl.BlockSpec(lbl.shape, lambda i, j: (0, 0)),
                  _resident((d, IN_COLS), lambda i, j: (0, 0))],
        out_specs=[tok(HGRN_WIDTH)] * 7 + [tok(POOL_WIDTH), tok(d), tok(d)],
        out_shape=[wide(BF16), wide(BF16), wide(BF16), wide(F32), wide(BF16), wide(F32), wide(BF16),
                   wide(F32, POOL_WIDTH), wide(BF16, d), wide(BF16, d)],
        compiler_params=_params(("arbitrary", "arbitrary")),
        name="in_proj",
    )(x, norm1_g, mod, lbl, w_in_bf)


def _reads(i, j, reverse):
    return i <= j if reverse else i >= j


def _block_edges(b_ref, reverse):
    nb = HGRN_CHUNK // HGRN_BLOCK
    zero = jnp.zeros((1, HGRN_WIDTH), F32)
    out = []
    for j in range(nb):
        if reverse:
            r = (j + 1) * HGRN_BLOCK
            out.append(zero if j == nb - 1 else b_ref[r:r + 1, :])
        else:
            r = j * HGRN_BLOCK
            out.append(zero if j == 0 else b_ref[r - 1:r, :])
    return out


def _seen(reverse):
    t = HGRN_CHUNK
    row = lax.broadcasted_iota(I32, (t, t), 0)
    col = lax.broadcasted_iota(I32, (t, t), 1)
    return (col >= row) if reverse else (col <= row)


def _worst_block_decay(g_ref):
    worst = None
    for lo in range(0, HGRN_CHUNK, HGRN_BLOCK):
        dec = jnp.sum(g_ref[0, lo:lo + HGRN_BLOCK, :], axis=0, keepdims=True)
        worst = dec if worst is None else jnp.minimum(worst, dec)
    return jnp.min(worst)


def _hgrn_decay(g_ref, b_ref, reverse):
    g = g_ref[0]
    hi = g.astype(BF16)
    lo = (g - hi.astype(F32)).astype(BF16)
    u01 = _one_zero(_seen(reverse))
    b_ref[...] = _dot(u01, hi) + _dot(u01, lo)


def _chunk_end(b_ref, reverse):
    return b_ref[0:1, :] if reverse else b_ref[HGRN_CHUNK - 1:HGRN_CHUNK, :]


def _decay_columns(b_end):
    e_end = jnp.exp(b_end)
    rows = [e_end[:, h * HEAD_DIM:(h + 1) * HEAD_DIM] for h in range(N_HEADS)]
    return jnp.concatenate(rows + [jnp.zeros((HEAD_DIM - N_HEADS, HEAD_DIM), F32)], axis=0).T


def _hgrn_fast(q_ref, v_ref, k_ref, b_ref, o_ref, s_ref, reverse):
    t, bl = HGRN_CHUNK, HGRN_BLOCK
    nb = t // bl
    seen = _seen(reverse)
    edges = _block_edges(b_ref, reverse)
    gain = {(i, j): jnp.exp(edges[i] - edges[j])
            for i in range(nb) for j in range(nb) if i != j and _reads(i, j, reverse)}
    b_end = _chunk_end(b_ref, reverse)
    e_cols = _decay_columns(b_end)
    zblk = jnp.zeros((bl, HEAD_DIM), BF16)
    full = nb - 1 if reverse else 0
    heads = [slice(h * HEAD_DIM, (h + 1) * HEAD_DIM) for h in range(N_HEADS)]

    scores, q_full, k_state = [], [], []
    for hs in heads:
        binc = b_ref[:, hs]
        q = q_ref[0, :, hs].astype(F32)
        k = k_ref[0, :, hs].astype(F32)
        q_own, k_own = [], []
        for i in range(nb):
            bs = slice(i * bl, (i + 1) * bl)
            q_own.append(q[bs] * jnp.exp(binc[bs] - edges[i][:, hs]))
            k_own.append((k[bs] * jnp.exp(edges[i][:, hs] - binc[bs])).astype(BF16))
        q_slabs = []
        for j in range(nb):
            parts = []
            for i in range(nb):
                if not _reads(i, j, reverse):
                    parts.append(zblk)
                elif i == j:
                    parts.append(q_own[i].astype(BF16))
                else:
                    parts.append((q_own[i] * gain[i, j][:, hs]).astype(BF16))
            q_slabs.append(jnp.concatenate(parts, axis=0))
        k_slabs = [jnp.concatenate([k_own[i] if i == j else zblk for i in range(nb)], axis=0)
                   for j in range(nb)]
        near = slice(nb // 2, nb) if reverse else slice(0, nb // 2)
        near_rows = slice(t // 2, t) if reverse else slice(0, t // 2)
        far_rows = slice(0, t // 2) if reverse else slice(t // 2, t)
        s_near = _dot_nt(jnp.concatenate([s[near_rows] for s in q_slabs[near]], axis=1),
                         jnp.concatenate(k_slabs[near], axis=1))
        s_far = _dot_nt(jnp.concatenate([s[far_rows] for s in q_slabs], axis=1),
                        jnp.concatenate(k_slabs, axis=1))
        scores.append(jnp.concatenate([s_far, s_near] if reverse else [s_near, s_far], axis=0))
        q_full.append(q_slabs[full])
        k_state.append((k * jnp.exp(b_end[:, hs] - binc)).astype(BF16))

    for h, hs in enumerate(heads):
        p = jnp.where(seen, scores[h], 0.0).astype(BF16)
        lhs = jnp.concatenate([p, q_full[h]], axis=1)
        rhs = jnp.concatenate([v_ref[0, :, hs], s_ref[h].astype(BF16)], axis=0)
        o_ref[0, :, hs] = _dot(lhs, rhs)

    for h, hs in enumerate(heads):
        s_ref[h] = s_ref[h] * e_cols[:, h:h + 1] + _dot_tn(k_state[h], v_ref[0, :, hs])


def _hgrn_slow(q_ref, v_ref, k_ref, b_ref, o_ref, s_ref, oi_ref, kk_ref, vv_ref, reverse):
    t = HGRN_CHUNK
    q = q_ref[0].astype(F32)
    binc = b_ref[...]
    kk_ref[...] = k_ref[0].astype(F32)
    vv_ref[...] = v_ref[0].astype(F32)
    oi_ref[...] = jnp.zeros((t, HGRN_WIDTH), F32)
    trow = lax.broadcasted_iota(I32, (t, HGRN_WIDTH), 0)
    heads = [slice(h * HEAD_DIM, (h + 1) * HEAD_DIM) for h in range(N_HEADS)]

    def body(s, carry):
        reads = (trow <= s) if reverse else (trow >= s)
        w = jnp.exp(jnp.where(reads, binc - b_ref[pl.ds(s, 1), :], 0.0))
        prod = jnp.where(reads, q * kk_ref[pl.ds(s, 1), :] * w, 0.0)
        vs = vv_ref[pl.ds(s, 1), :]
        for hs in heads:
            oi_ref[:, hs] += jnp.sum(prod[:, hs], axis=1, keepdims=True) * vs[:, hs]
        return carry

    lax.fori_loop(0, t, body, 0)
    b_end = _chunk_end(b_ref, reverse)
    e_cols = _decay_columns(b_end)
    q_full = (q * jnp.exp(binc)).astype(BF16)
    k_state = (kk_ref[...] * jnp.exp(b_end - binc)).astype(BF16)
    for h, hs in enumerate(heads):
        o_ref[0, :, hs] = oi_ref[:, hs] + _dot(q_full[:, hs], s_ref[h].astype(BF16))
        s_ref[h] = s_ref[h] * e_cols[:, h:h + 1] + _dot_tn(k_state[:, hs], v_ref[0, :, hs])


def _hgrn_kernel(qf_ref, vf_ref, kf_ref, gf_ref, qb_ref, vb_ref, kb_ref, gb_ref, s0f_ref, s0b_ref,
                 of_ref, ob_ref, sf_ref, sb_ref, bf_ref, bb_ref, oi_ref, kk_ref, vv_ref):
    @pl.when(pl.program_id(1) == 0)
    def _():
        sf_ref[...] = s0f_ref[0]
        sb_ref[...] = s0b_ref[0]

    safe = jnp.minimum(_worst_block_decay(gf_ref), _worst_block_decay(gb_ref)) >= -HGRN_SAFE_DECAY

    @pl.when(safe)
    def _():
        _hgrn_decay(gf_ref, bf_ref, False)
        _hgrn_decay(gb_ref, bb_ref, True)
        _hgrn_fast(qf_ref, vf_ref, kf_ref, bf_ref, of_ref, sf_ref, False)
        _hgrn_fast(qb_ref, vb_ref, kb_ref, bb_ref, ob_ref, sb_ref, True)

    @pl.when(jnp.logical_not(safe))
    def _():
        _hgrn_decay(gf_ref, bf_ref, False)
        _hgrn_decay(gb_ref, bb_ref, True)
        _hgrn_slow(qf_ref, vf_ref, kf_ref, bf_ref, of_ref, sf_ref, oi_ref, kk_ref, vv_ref, False)
        _hgrn_slow(qb_ref, vb_ref, kb_ref, bb_ref, ob_ref, sb_ref, oi_ref, kk_ref, vv_ref, True)


def _hgrn(q, v, kf, gf, kb, gb, s0f, s0b):
    b, l, w = q.shape
    t = HGRN_CHUNK
    n = l // t
    fwd = pl.BlockSpec((1, t, w), lambda i, j: (i, j, 0))
    bwd = pl.BlockSpec((1, t, w), lambda i, j: (i, n - 1 - j, 0))
    sspec = pl.BlockSpec((1, N_HEADS, HEAD_DIM, HEAD_DIM), lambda i, j: (i, 0, 0, 0))
    out = jax.ShapeDtypeStruct((b, l, w), F32)
    state = pltpu.VMEM((N_HEADS, HEAD_DIM, HEAD_DIM), F32)
    rows = pltpu.VMEM((t, w), F32)
    return pl.pallas_call(
        _hgrn_kernel,
        grid=(b, n),
        in_specs=[fwd, fwd, fwd, fwd, bwd, bwd, bwd, bwd, sspec, sspec],
        out_specs=[fwd, bwd],
        out_shape=[out, out],
        scratch_shapes=[state, state, rows, rows, rows, rows, rows],
        compiler_params=_params(("arbitrary", "arbitrary")),
        name="hgrn",
    )(q, v, kf, gf, q, v, kb, gb, s0f, s0b)


def _pool_col_matrices():
    mats = np.zeros((len(POOL_WINDOWS), POOL_TILE, POOL_TILE), np.float32)
    for j, w in enumerate(POOL_WINDOWS):
        for t in range(POOL_TILE):
            r, c = divmod(t, GRID_W)
            lo, hi = max(c - w // 2, 0), min(c + w // 2 - 1, GRID_W - 1)
            mats[j, t, r * GRID_W + lo:r * GRID_W + hi + 1] = 1.0
    return jnp.asarray(mats, BF16)


def _window_len(pos, half, n):
    return jnp.minimum(pos + half - 1, n - 1) + 1 - jnp.maximum(pos - half, 0)


def _pool_kernel(p_ref, a_ref, wp_ref, ps_ref, o_ref, pad_ref, inr_ref, inc_ref, *, rows):
    j = pl.program_id(1)
    half = jnp.left_shift(1, j)
    gw = GRID_W
    halo = POOL_HALO * gw
    l = rows * gw
    tile_rows = POOL_TILE // gw
    edge = POOL_HALO // 2
    for buf in range(2):
        pad_ref[buf, 0:halo, :] = jnp.zeros((halo, POOL_GROUP), F32)
        pad_ref[buf, halo + l:2 * halo + l, :] = jnp.zeros((halo, POOL_GROUP), F32)
    pad_ref[0, halo:halo + l, :] = p_ref[0]

    def level(src, dst, back, fwd):
        def body(g, carry):
            for u in range(8):
                t = pl.multiple_of((edge + g * 8 + u) * gw, gw)
                pad_ref[dst, pl.ds(t, gw), :] = (pad_ref[src, pl.ds(t - back * gw, gw), :]
                                                 + pad_ref[src, pl.ds(t + fwd * gw, gw), :])
            return carry

        lax.fori_loop(0, (rows + 2 * (POOL_HALO - edge)) // 8, body, 0)

    level(0, 1, 1, 0)

    @pl.when(j >= 1)
    def _():
        level(1, 0, 1, 1)

    @pl.when(j >= 2)
    def _():
        level(0, 1, 2, 2)

    @pl.when(j >= 3)
    def _():
        level(1, 0, 4, 4)

    summed = 1 - jnp.bitwise_and(j, 1)
    r = lax.broadcasted_iota(I32, inr_ref.shape, 0)
    inr_ref[...] = 1.0 / _window_len(r, half, rows).astype(F32)
    c = jnp.bitwise_and(lax.broadcasted_iota(I32, inc_ref.shape, 0), gw - 1)
    inc_ref[...] = 1.0 / _window_len(c, half, gw).astype(F32)
    a = a_ref[0]
    wp = wp_ref[0]
    scale = ps_ref[0]

    def tile(i):
        t0 = pl.multiple_of(i * POOL_TILE, POOL_TILE)
        rsum = pad_ref[summed, pl.ds(halo + t0, POOL_TILE), :]
        hi = rsum.astype(BF16)
        lo = (rsum - hi.astype(F32)).astype(BF16)
        total = _dot(a, hi) + _dot(a, lo)
        by_row = [total[rr * gw:(rr + 1) * gw] * inr_ref[pl.ds(i * tile_rows + rr, 1), :]
                  for rr in range(tile_rows)]
        resid = jnp.concatenate(by_row, axis=0) * inc_ref[...] - p_ref[0, pl.ds(t0, POOL_TILE), :]
        o_ref[0, pl.ds(t0, POOL_TILE), :] = (_dot(resid.astype(BF16), wp) * scale).astype(BF16)

    def step(i, carry):
        for u in range(POOL_UNROLL):
            tile(POOL_UNROLL * i + u)
        return carry

    lax.fori_loop(0, l // (POOL_UNROLL * POOL_TILE), step, 0)


def _pool(p, w_pool_bf, pool_scale):
    b, l, _ = p.shape
    rows = l // GRID_W
    ng = len(POOL_WINDOWS)
    grp = pl.BlockSpec((1, l, POOL_GROUP), lambda i, j: (i, 0, j))
    return pl.pallas_call(
        functools.partial(_pool_kernel, rows=rows),
        grid=(b, ng),
        in_specs=[grp,
                  pl.BlockSpec((1, POOL_TILE, POOL_TILE), lambda i, j: (j, 0, 0)),
                  pl.BlockSpec((1, POOL_GROUP, POOL_GROUP), lambda i, j: (j, 0, 0)),
                  pl.BlockSpec((1, 1, POOL_GROUP), lambda i, j: (j, 0, 0))],
        out_specs=grp,
        out_shape=jax.ShapeDtypeStruct((b, l, POOL_WIDTH), BF16),
        scratch_shapes=[pltpu.VMEM((2, (rows + 2 * POOL_HALO) * GRID_W, POOL_GROUP), F32),
                        pltpu.VMEM((rows, POOL_GROUP), F32),
                        pltpu.VMEM((POOL_TILE, POOL_GROUP), F32)],
        compiler_params=_params(("arbitrary", "arbitrary")),
        name="pool",
    )(p, _pool_col_matrices(), w_pool_bf, pool_scale.reshape(ng, 1, POOL_GROUP))


def _merge_kernel(x_ref, of_ref, ob_ref, og_ref, pl_ref, ga_ref, gb_ref, mod_ref, hg_ref, n2_ref,
                  wa_ref, wb_ref, wo_ref, wr_ref, nh_ref, vx_ref, pr_ref):
    b = pl.program_id(0)
    m = mod_ref[pl.ds(b, 1), :]
    g1 = m[:, 2 * D_MODEL:3 * D_MODEL]
    sh2 = m[:, 3 * D_MODEL:4 * D_MODEL]
    sc2 = m[:, 4 * D_MODEL:5 * D_MODEL]
    o = of_ref[0] + ob_ref[0]
    hg = hg_ref[...]
    og = og_ref[0]
    gated = []
    for h in range(N_HEADS):
        hs = slice(h * HEAD_DIM, (h + 1) * HEAD_DIM)
        gated.append((_rms(o[:, hs], hg) * og[:, hs].astype(F32)).astype(BF16))
    ya = _dot(jnp.concatenate(gated, axis=1), wa_ref[...])
    yb = _dot(pl_ref[0], wb_ref[...])
    y = ga_ref[0].astype(F32) * ya + gb_ref[0].astype(F32) * yb
    nh = x_ref[0] + g1 * _dot(y.astype(BF16), wo_ref[...])
    nh_ref[0] = nh
    vx = (_rms(nh, n2_ref[...]) * (1.0 + sc2) + sh2).astype(BF16)
    vx_ref[0] = vx
    logits = _dot_nt(wr_ref[...], vx)
    ex = jnp.exp(logits - jnp.max(logits, axis=0, keepdims=True))
    pr_ref[0] = ex / jnp.sum(ex, axis=0, keepdims=True)


def _merge(x, o_f, o_b, og, pooled, ga, gb, mod, hgrn_g, norm2_g, w_a_bf, w_b_bf, w_out_bf, w_router_t):
    b, l, d = x.shape
    tm = PROJ_TILE
    tok = lambda w: pl.BlockSpec((1, tm, w), lambda i, j: (i, j, 0))
    const = lambda a: pl.BlockSpec(a.shape, lambda i, j: (0,) * a.ndim)
    return pl.pallas_call(
        _merge_kernel,
        grid=(b, l // tm),
        in_specs=[tok(d), tok(d), tok(d), tok(d), tok(POOL_WIDTH), tok(d), tok(d),
                  const(mod), const(hgrn_g), const(norm2_g),
                  _resident(w_a_bf.shape, lambda i, j: (0, 0)),
                  _resident(w_b_bf.shape, lambda i, j: (0, 0)),
                  _resident(w_out_bf.shape, lambda i, j: (0, 0)),
                  const(w_router_t)],
        out_specs=[tok(d), tok(d), pl.BlockSpec((1, N_EXPERTS, tm), lambda i, j: (i, 0, j))],
        out_shape=[jax.ShapeDtypeStruct((b, l, d), F32),
                   jax.ShapeDtypeStruct((b, l, d), BF16),
                   jax.ShapeDtypeStruct((b, N_EXPERTS, l), F32)],
        compiler_params=_params(("arbitrary", "arbitrary")),
        name="merge",
    )(x, o_f, o_b, og, pooled, ga, gb, mod, hgrn_g, norm2_g, w_a_bf, w_b_bf, w_out_bf, w_router_t)


def _route_kernel(p_ref, code_ref, start_ref, *, cap):
    l = p_ref.shape[2]
    tm = TOKEN_TILE
    bits = lax.bitcast_convert_type(p_ref[0], I32)
    capf = jnp.float32(cap)

    def count_ge(cand):
        return jnp.sum(jnp.where(bits >= cand, 1.0, 0.0), axis=1, keepdims=True)

    def search(i, prefix):
        cand = jnp.bitwise_or(prefix, jnp.left_shift(1, 30 - i))
        return jnp.where(count_ge(cand) >= capf, cand, prefix)

    tau = lax.fori_loop(0, 31, search, jnp.zeros((N_EXPERTS, 1), I32))
    need = capf - jnp.sum(jnp.where(bits > tau, 1.0, 0.0), axis=1, keepdims=True)
    before = _one_zero(lax.broadcasted_iota(I32, (tm, tm), 0) < lax.broadcasted_iota(I32, (tm, tm), 1))
    lane = lax.broadcasted_iota(I32, (N_EXPERTS, 128), 1)

    def tile(k, carry):
        n_gt, n_eq, starts = carry
        sl = pl.ds(pl.multiple_of(k * tm, tm), tm)
        bk = lax.bitcast_convert_type(p_ref[0, :, sl], I32)
        gt = bk > tau
        eq = bk == tau
        gt01 = _one_zero(gt)
        eq01 = _one_zero(eq)
        eq_before = n_eq + _dot(eq01, before)
        chosen = jnp.logical_or(gt, jnp.logical_and(eq, eq_before < need))
        rank = n_gt + _dot(gt01, before) + jnp.minimum(eq_before, need)
        code_ref[0, :, sl] = jnp.where(chosen, rank, -1.0).astype(I32)
        starts = jnp.where(lane == k, (n_gt + jnp.minimum(n_eq, need)).astype(I32), starts)
        n_gt = n_gt + jnp.sum(gt01.astype(F32), axis=1, keepdims=True)
        n_eq = n_eq + jnp.sum(eq01.astype(F32), axis=1, keepdims=True)
        return n_gt, n_eq, starts

    zero = jnp.zeros((N_EXPERTS, 1), F32)
    _, _, starts = lax.fori_loop(0, l // tm, tile, (zero, zero, jnp.zeros((N_EXPERTS, 128), I32)))
    start_ref[0] = jnp.where(lane == l // tm, cap, starts)


def _route(probs_t, cap):
    b, e, l = probs_t.shape
    assert l // TOKEN_TILE < 128
    return pl.pallas_call(
        functools.partial(_route_kernel, cap=cap),
        grid=(b,),
        in_specs=[pl.BlockSpec((1, e, l), lambda i: (i, 0, 0))],
        out_specs=[pl.BlockSpec((1, e, l), lambda i: (i, 0, 0)),
                   pl.BlockSpec((1, e, 128), lambda i: (i, 0, 0))],
        out_shape=[jax.ShapeDtypeStruct((b, e, l), I32), jax.ShapeDtypeStruct((b, e, 128), I32)],
        compiler_params=_params(("arbitrary",)),
        name="route",
    )(probs_t)


def _moe_kernel(start_ref, vx_ref, code_ref, p_ref, wg_ref, wu_ref, wd_ref, y_ref, xg_ref, ps_ref,
                *, cap, n_tiles):
    b = pl.program_id(0)
    e = pl.program_id(1)
    tm = TOKEN_TILE
    w = SLOT_WINDOW
    base_idx = (b * N_EXPERTS + e) * 128
    xg_ref[...] = jnp.zeros(xg_ref.shape, F32)
    ps_ref[...] = jnp.zeros(ps_ref.shape, F32)
    slot = lax.broadcasted_iota(I32, (w, tm), 0)
    unroll = min(GATHER_UNROLL, n_tiles)
    assert n_tiles % unroll == 0

    def gather(k, base):
        tsl = pl.ds(pl.multiple_of(k * tm, tm), tm)
        codes = code_ref[0, pl.ds(e, 1), tsl]
        hit = codes == (slot + base)
        rows_x = _dot(_one_zero(hit), vx_ref[0, tsl, :])
        rows_p = jnp.sum(jnp.where(hit, p_ref[0, pl.ds(e, 1), tsl], 0.0), axis=1, keepdims=True)
        return rows_x, rows_p

    def add(base, rows_x, rows_p):
        xg_ref[pl.ds(base, w), :] += rows_x
        ps_ref[pl.ds(base, w), :] += rows_p

    def tiles(g, carry):
        first = []
        for u in range(unroll):
            k = g * unroll + u
            st = start_ref[base_idx + k]
            end = start_ref[base_idx + k + 1]
            base = pl.multiple_of(jnp.left_shift(jnp.right_shift(st, 3), 3), 8)
            first.append((k, base, end) + gather(k, base))
        for k, base, end, rows_x, rows_p in first:
            add(base, rows_x, rows_p)
        for k, base, end, _, _ in first:
            def more(c, carry2, k=k, base=base):
                b2 = pl.multiple_of(base + c * w, 8)
                add(b2, *gather(k, b2))
                return carry2

            lax.fori_loop(1, jnp.right_shift(end - base + w - 1, SLOT_SHIFT), more, 0)
        return carry

    lax.fori_loop(0, n_tiles // unroll, tiles, 0)
    fr = min(FFN_ROWS, cap)
    for r0 in range(0, cap, fr):
        xg = xg_ref[r0:r0 + fr, :].astype(BF16)
        hid = (_silu(_dot(xg, wg_ref[0])) * _dot(xg, wu_ref[0])).astype(BF16)
        y_ref[0, 0, r0:r0 + fr, :] = (_dot(hid, wd_ref[0]) * ps_ref[r0:r0 + fr, 0:1]).astype(BF16)


def _moe(starts_flat, vx, code, probs_t, wg_bf, wu_bf, wd_bf, cap):
    b, l, d = vx.shape
    n_tiles = l // TOKEN_TILE
    grid_spec = pltpu.PrefetchScalarGridSpec(
        num_scalar_prefetch=1,
        grid=(b, N_EXPERTS),
        in_specs=[_resident((1, l, d), lambda i, e, s: (i, 0, 0)),
                  pl.BlockSpec((1, N_EXPERTS, l), lambda i, e, s: (i, 0, 0)),
                  pl.BlockSpec((1, N_EXPERTS, l), lambda i, e, s: (i, 0, 0)),
                  pl.BlockSpec((1, d, EXPERT_FF), lambda i, e, s: (e, 0, 0)),
                  pl.BlockSpec((1, d, EXPERT_FF), lambda i, e, s: (e, 0, 0)),
                  pl.BlockSpec((1, EXPERT_FF, d), lambda i, e, s: (e, 0, 0))],
        out_specs=pl.BlockSpec((1, 1, cap, d), lambda i, e, s: (i, e, 0, 0)),
        scratch_shapes=[pltpu.VMEM((cap + SLOT_WINDOW, d), F32),
                        pltpu.VMEM((cap + SLOT_WINDOW, 128), F32)],
    )
    return pl.pallas_call(
        functools.partial(_moe_kernel, cap=cap, n_tiles=n_tiles),
        grid_spec=grid_spec,
        out_shape=jax.ShapeDtypeStruct((b, N_EXPERTS, cap, d), BF16),
        compiler_params=_params(("arbitrary", "arbitrary")),
        name="moe",
    )(starts_flat, vx, code, probs_t, wg_bf, wu_bf, wd_bf)


def _combine_kernel(start_ref, nh_ref, code_ref, mod_ref, fg_ref, y_hbm, o_ref, ybuf, sem,
                    *, cap, n_tiles, n_steps):
    b = pl.program_id(0)
    k = pl.program_id(1)
    step = b * n_tiles + k
    tm = TOKEN_TILE
    w = SLOT_WINDOW

    def window(bb, kk, e, c):
        st = start_ref[(bb * N_EXPERTS + e) * 128 + kk]
        lo = jnp.left_shift(jnp.right_shift(st, 3), 3) + c * w
        return lo, pl.multiple_of(jnp.minimum(lo, cap - w), 8)

    def copies(bb, kk, buf):
        out = []
        for e in range(N_EXPERTS):
            _, src = window(bb, kk, e, 0)
            out.append(pltpu.make_async_copy(y_hbm.at[bb, e, pl.ds(src, w), :], ybuf.at[buf, e], sem.at[buf, e]))
        return out

    @pl.when(step == 0)
    def _():
        for cp in copies(b, k, 0):
            cp.start()

    cur = lax.rem(step, 2)

    @pl.when(step + 1 < n_steps)
    def _():
        nk = jnp.where(k + 1 == n_tiles, 0, k + 1)
        nb = jnp.where(k + 1 == n_tiles, b + 1, b)
        for cp in copies(nb, nk, 1 - cur):
            cp.start()

    for cp in copies(b, k, cur):
        cp.wait()

    slot = lax.broadcasted_iota(I32, (w, tm), 0)

    def scatter(c, buf_rows):
        hits = []
        for e in range(N_EXPERTS):
            lo, src = window(b, k, e, c)
            codes = code_ref[0, e:e + 1, :]
            hit = jnp.logical_and(codes == (slot + src), jnp.logical_and(codes >= lo, codes < lo + w))
            hits.append(_one_zero(hit))
        return _dot_tn(jnp.concatenate(hits, axis=0), buf_rows)

    acc = scatter(0, ybuf[cur].reshape(N_EXPERTS * w, D_MODEL))

    most = jnp.int32(0)
    for e in range(N_EXPERTS):
        i0 = (b * N_EXPERTS + e) * 128 + k
        st = start_ref[i0]
        span = st - jnp.left_shift(jnp.right_shift(st, 3), 3) + start_ref[i0 + 1] - st
        most = jnp.maximum(most, jnp.right_shift(span + w - 1, SLOT_SHIFT))

    def extra(c, acc):
        cps = []
        for e in range(N_EXPERTS):
            _, src = window(b, k, e, c)
            cps.append(pltpu.make_async_copy(y_hbm.at[b, e, pl.ds(src, w), :], ybuf.at[cur, e], sem.at[cur, e]))
        for cp in cps:
            cp.start()
        for cp in cps:
            cp.wait()
        return acc + scatter(c, ybuf[cur].reshape(N_EXPERTS * w, D_MODEL))

    acc = lax.fori_loop(1, most, extra, acc)
    g2 = mod_ref[pl.ds(b, 1), 5 * D_MODEL:6 * D_MODEL]
    o_ref[0] = _rms(nh_ref[0] + g2 * acc, fg_ref[...])


def _combine(starts_flat, new_hx, code, mod, final_g, y, cap):
    b, l, d = new_hx.shape
    tm = TOKEN_TILE
    n_tiles = l // tm
    grid_spec = pltpu.PrefetchScalarGridSpec(
        num_scalar_prefetch=1,
        grid=(b, n_tiles),
        in_specs=[pl.BlockSpec((1, tm, d), lambda i, j, s: (i, j, 0)),
                  pl.BlockSpec((1, N_EXPERTS, tm), lambda i, j, s: (i, 0, j)),
                  pl.BlockSpec(mod.shape, lambda i, j, s: (0, 0)),
                  pl.BlockSpec((1, d), lambda i, j, s: (0, 0)),
                  pl.BlockSpec(memory_space=pl.ANY)],
        out_specs=pl.BlockSpec((1, tm, d), lambda i, j, s: (i, j, 0)),
        scratch_shapes=[pltpu.VMEM((2, N_EXPERTS, SLOT_WINDOW, d), BF16),
                        pltpu.SemaphoreType.DMA((2, N_EXPERTS))],
    )
    return pl.pallas_call(
        functools.partial(_combine_kernel, cap=cap, n_tiles=n_tiles, n_steps=b * n_tiles),
        grid_spec=grid_spec,
        out_shape=jax.ShapeDtypeStruct((b, l, d), F32),
        compiler_params=_params(("arbitrary", "arbitrary")),
        name="combine",
    )(starts_flat, new_hx, code, mod, final_g, y)


def _layer(x, c, ctx, c_ctx, lb_logits, w_mod, b_mod, norm1_g, w_in, hgrn_norm_g, w_a, w_pool, pool_scale,
           w_b, w_out, norm2_g, w_router, w_e_gate, w_e_up, w_e_down, final_g):
    b, l, d = x.shape
    cap = CAPACITY_FACTOR * l // N_EXPERTS
    assert b < MOD_ROWS and l % HGRN_CHUNK == 0 and l % PROJ_TILE == 0 and l % TOKEN_TILE == 0
    assert cap % 8 == 0 and cap >= SLOT_WINDOW and l % (POOL_UNROLL * POOL_TILE) == 0
    cc = jnp.zeros((MOD_ROWS, d), F32).at[:b].set(c).at[b].set(c_ctx)
    assert lb_logits.shape == (2, 2, HGRN_WIDTH)
    lbl = lb_logits.reshape(4, HGRN_WIDTH)
    row = lambda a: a.reshape(1, -1)
    w_in_bf = w_in.astype(BF16)

    mod = _adaln(cc, w_mod, b_mod)
    s0f, s0b = _ctx_states(ctx, row(norm1_g), mod, lbl, w_in_bf)
    q, v, kf, gf, kb, gb, og, p, ga, gbm = _in_proj(x, row(norm1_g), mod, lbl, w_in_bf)
    o_f, o_b = _hgrn(q, v, kf, gf, kb, gb, s0f, s0b)
    pooled = _pool(p, w_pool.astype(BF16), pool_scale)
    new_hx, vx, probs_t = _merge(x, o_f, o_b, og, pooled, ga, gbm, mod, row(hgrn_norm_g), row(norm2_g),
                                 w_a.astype(BF16), w_b.astype(BF16), w_out.astype(BF16),
                                 w_router.T.astype(BF16))
    code, starts = _route(probs_t, cap)
    starts_flat = starts.reshape(-1)
    y = _moe(starts_flat, vx, code, probs_t, w_e_gate.astype(BF16), w_e_up.astype(BF16),
             w_e_down.astype(BF16), cap)
    return _combine(starts_flat, new_hx, code, mod, row(final_g), y, cap)


def kernel(x, c, ctx, c_ctx, lb_logits, w_mod, b_mod, norm1_g, w_in, hgrn_norm_g, w_a, w_pool, pool_scale,
           w_b, w_out, norm2_g, w_router, w_e_gate, w_e_up, w_e_down, final_g):
    assert w_mod.shape[0] == 1, "single-layer trunk: the context stream only seeds the latent recurrence"
    return _layer(x, c, ctx, c_ctx, lb_logits, w_mod[0], b_mod[0], norm1_g[0], w_in[0], hgrn_norm_g[0], w_a[0],
                  w_pool[0], pool_scale[0], w_b[0], w_out[0], norm2_g[0], w_router[0], w_e_gate[0],
                  w_e_up[0], w_e_down[0], final_g)
```

```python
import functools

import numpy as np
import jax
import jax.numpy as jnp
from jax import lax
from jax.experimental import pallas as pl
from jax.experimental.pallas import tpu as pltpu

F32 = jnp.float32
BF16 = jnp.bfloat16
I32 = jnp.int32

D_MODEL = 1024
N_HEADS = 8
HEAD_DIM = 128
HGRN_WIDTH = N_HEADS * HEAD_DIM
GRID_W = 64
POOL_WINDOWS = (2, 4, 8, 16)
POOL_GROUP = 128
POOL_WIDTH = POOL_GROUP * len(POOL_WINDOWS)
N_EXPERTS = 16
CAPACITY_FACTOR = 2
EXPERT_FF = 1024
EPS = 1e-6

Q_OFF = 0
I_OFF = Q_OFF + HGRN_WIDTH
FF_OFF = I_OFF + HGRN_WIDTH
FB_OFF = FF_OFF + HGRN_WIDTH
OG_OFF = FB_OFF + HGRN_WIDTH
P_OFF = OG_OFF + HGRN_WIDTH
GA_OFF = P_OFF + POOL_WIDTH
GB_OFF = GA_OFF + D_MODEL
IN_COLS = GB_OFF + D_MODEL

MOD_ROWS = 16
PROJ_TILE = 512
MERGE_ROWS = 512
TOKEN_TILE = 256
HGRN_CHUNK = 128
HGRN_BLOCK = 32
HGRN_SAFE_DECAY = 80.0
POOL_HALO = 16
POOL_TILE = 256
POOL_UNROLL = 4
SLOT_SHIFT = 6
SLOT_WINDOW = 1 << SLOT_SHIFT
FFN_ROWS = 256
VMEM_LIMIT = 56 * 1024 * 1024


def _dot(a, b):
    return jnp.dot(a, b, preferred_element_type=F32)


def _dot_nt(a, b):
    return lax.dot_general(a, b, (((1,), (1,)), ((), ())), preferred_element_type=F32)


def _dot_tn(a, b):
    return lax.dot_general(a, b, (((0,), (0,)), ((), ())), preferred_element_type=F32)


def _rms(x, g):
    ms = jnp.mean(x * x, axis=-1, keepdims=True)
    return x * lax.rsqrt(ms + EPS) * g


def _silu(z):
    return z * jax.nn.sigmoid(z)


def _lower_bound(a0, a1):
    m = jnp.maximum(a0, a1)
    e0 = jnp.exp(a0 - m)
    e1 = jnp.exp(a1 - m)
    return e0 / (e0 + e1)


def _split3(g):
    hi = g.astype(BF16)
    r = g - hi.astype(F32)
    mid = r.astype(BF16)
    lo = (r - mid.astype(F32)).astype(BF16)
    return hi, mid, lo


def _apply01(u01, g):
    hi, mid, lo = _split3(g)
    return _dot(u01, hi) + _dot(u01, mid) + _dot(u01, lo)


def _one_zero(mask):
    return jnp.where(mask, 1.0, 0.0).astype(BF16)


def _params(sem):
    return pltpu.CompilerParams(dimension_semantics=sem, vmem_limit_bytes=VMEM_LIMIT)


def _resident(shape, index_map):
    return pl.BlockSpec(shape, index_map, pipeline_mode=pl.Buffered(1))


def _adaln_kernel(c_ref, w_ref, b_ref, o_ref):
    c = c_ref[...]
    o_ref[...] = _dot(_silu(c).astype(BF16), w_ref[...].astype(BF16)) + b_ref[...]


def _adaln(cc, w_mod, b_mod):
    n = w_mod.shape[1]
    tn = 768
    return pl.pallas_call(
        _adaln_kernel,
        grid=(n // tn,),
        in_specs=[pl.BlockSpec((MOD_ROWS, D_MODEL), lambda j: (0, 0)),
                  pl.BlockSpec((D_MODEL, tn), lambda j: (0, j)),
                  pl.BlockSpec((1, tn), lambda j: (0, j))],
        out_specs=pl.BlockSpec((MOD_ROWS, tn), lambda j: (0, j)),
        out_shape=jax.ShapeDtypeStruct((MOD_ROWS, n), F32),
        compiler_params=_params(("arbitrary",)),
        name="adaln",
    )(cc, w_mod, b_mod.reshape(1, n))


def _ctx_kernel(ctx_ref, g1_ref, mod_ref, lbl_ref, wi_ref, wf_ref, wb_ref, sf_ref, sb_ref, *, ctx_row):
    x = ctx_ref[0]
    n = x.shape[0]
    sh = mod_ref[ctx_row:ctx_row + 1, 0:D_MODEL]
    sc = mod_ref[ctx_row:ctx_row + 1, D_MODEL:2 * D_MODEL]
    u = (_rms(x, g1_ref[...]) * (1.0 + sc) + sh).astype(BF16)
    v = _dot(u, wi_ref[...]).astype(BF16)
    zf = _dot(u, wf_ref[...])
    zb = _dot(u, wb_ref[...])
    lbf = _lower_bound(lbl_ref[0:1, :], lbl_ref[1:2, :])
    lbb = _lower_bound(lbl_ref[2:3, :], lbl_ref[3:4, :])
    ff = lbf + (1.0 - lbf) * jax.nn.sigmoid(zf)
    fb = lbb + (1.0 - lbb) * jax.nn.sigmoid(zb)
    r = lax.broadcasted_iota(I32, (n, n), 0)
    c = lax.broadcasted_iota(I32, (n, n), 1)
    ef = _apply01(_one_zero(c > r), jnp.log(ff))
    eb = _apply01(_one_zero(c < r), jnp.log(fb))
    kf = ((1.0 - ff) * jnp.exp(ef)).astype(BF16)
    kb = ((1.0 - fb) * jnp.exp(eb)).astype(BF16)
    for h in range(N_HEADS):
        hs = slice(h * HEAD_DIM, (h + 1) * HEAD_DIM)
        sf_ref[0, h] = _dot_tn(kf[:, hs], v[:, hs])
        sb_ref[0, h] = _dot_tn(kb[:, hs], v[:, hs])


def _ctx_states(ctx, norm1_g, mod, lbl, w_in_bf):
    b, lc, d = ctx.shape
    col = lambda k: pl.BlockSpec((d, HGRN_WIDTH), lambda i, k=k: (0, k))
    state = jax.ShapeDtypeStruct((b, N_HEADS, HEAD_DIM, HEAD_DIM), F32)
    sspec = pl.BlockSpec((1, N_HEADS, HEAD_DIM, HEAD_DIM), lambda i: (i, 0, 0, 0))
    return pl.pallas_call(
        functools.partial(_ctx_kernel, ctx_row=b),
        grid=(b,),
        in_specs=[pl.BlockSpec((1, lc, d), lambda i: (i, 0, 0)),
                  pl.BlockSpec((1, d), lambda i: (0, 0)),
                  pl.BlockSpec(mod.shape, lambda i: (0, 0)),
                  pl.BlockSpec(lbl.shape, lambda i: (0, 0)),
                  col(I_OFF // HGRN_WIDTH), col(FF_OFF // HGRN_WIDTH), col(FB_OFF // HGRN_WIDTH)],
        out_specs=[sspec, sspec],
        out_shape=[state, state],
        compiler_params=_params(("arbitrary",)),
        name="ctx_state",
    )(ctx, norm1_g, mod, lbl, w_in_bf, w_in_bf, w_in_bf)


def _chunks_safe(logf):
    out = []
    for c0 in range(0, logf.shape[0], HGRN_CHUNK):
        worst = None
        for lo in range(c0, c0 + HGRN_CHUNK, HGRN_BLOCK):
            dec = jnp.sum(logf[lo:lo + HGRN_BLOCK], axis=0, keepdims=True)
            worst = dec if worst is None else jnp.minimum(worst, dec)
        ok = jnp.min(worst, axis=1, keepdims=True) >= -HGRN_SAFE_DECAY
        out.append(jnp.broadcast_to(jnp.where(ok, 1.0, 0.0), (1, 128)))
    return out


def _inproj_kernel(x_ref, g1_ref, mod_ref, lbl_ref, w_ref,
                   q_ref, v_ref, kf_ref, gf_ref, kb_ref, gb_ref, og_ref, p_ref, ga_ref, gbm_ref, safe_ref):
    b = pl.program_id(0)
    x = x_ref[0]
    m = mod_ref[pl.ds(b, 1), :]
    sh = m[:, 0:D_MODEL]
    sc = m[:, D_MODEL:2 * D_MODEL]
    u = (_rms(x, g1_ref[...]) * (1.0 + sc) + sh).astype(BF16)

    z = _dot(u, w_ref[:, Q_OFF:I_OFF])
    q_ref[0] = _silu(z).astype(BF16)
    v_ref[0] = _dot(u, w_ref[:, I_OFF:FF_OFF]).astype(BF16)

    lbf = _lower_bound(lbl_ref[0:1, :], lbl_ref[1:2, :])
    f = lbf + (1.0 - lbf) * jax.nn.sigmoid(_dot(u, w_ref[:, FF_OFF:FB_OFF]))
    kf_ref[0] = (1.0 - f).astype(BF16)
    logf = jnp.log(f)
    gf_ref[0] = logf
    flags = _chunks_safe(logf)
    lbb = _lower_bound(lbl_ref[2:3, :], lbl_ref[3:4, :])
    f = lbb + (1.0 - lbb) * jax.nn.sigmoid(_dot(u, w_ref[:, FB_OFF:OG_OFF]))
    kb_ref[0] = (1.0 - f).astype(BF16)
    logf = jnp.log(f)
    gb_ref[0] = logf
    flags = flags + _chunks_safe(logf)
    safe_ref[0, 0] = jnp.concatenate(flags + [jnp.zeros((8 - len(flags), 128), F32)], axis=0)

    z = _dot(u, w_ref[:, OG_OFF:P_OFF])
    og_ref[0] = _silu(z).astype(BF16)
    p_ref[0] = _dot(u, w_ref[:, P_OFF:GA_OFF])
    ga_ref[0] = jax.nn.sigmoid(_dot(u, w_ref[:, GA_OFF:GB_OFF])).astype(BF16)
    gbm_ref[0] = jax.nn.sigmoid(_dot(u, w_ref[:, GB_OFF:IN_COLS])).astype(BF16)


def _in_proj(x, norm1_g, mod, lbl, w_in_bf):
    b, l, d = x.shape
    tm = TOKEN_TILE
    tok = lambda w: pl.BlockSpec((1, tm, w), lambda i, j: (i, j, 0))
    wide = lambda dt, w=HGRN_WIDTH: jax.ShapeDtypeStruct((b, l, w), dt)
    return pl.pallas_call(
        _inproj_kernel,
        grid=(b, l // tm),
        in_specs=[tok(d),
                  pl.BlockSpec((1, d), lambda i, j: (0, 0)),
                  pl.BlockSpec(mod.shape, lambda i, j: (0, 0)),
                  pl.BlockSpec(lbl.shape, lambda i, j: (0, 0)),
                  _resident((d, IN_COLS), lambda i, j: (0, 0))],
        out_specs=[tok(HGRN_WIDTH)] * 7 + [tok(POOL_WIDTH), tok(d), tok(d),
                                            pl.BlockSpec((1, 1, 8, 128), lambda i, j: (i, j, 0, 0))],
        out_shape=[wide(BF16), wide(BF16), wide(BF16), wide(F32), wide(BF16), wide(F32), wide(BF16),
                   wide(F32, POOL_WIDTH), wide(BF16, d), wide(BF16, d),
                   jax.ShapeDtypeStruct((b, l // tm, 8, 128), F32)],
        compiler_params=_params(("arbitrary", "arbitrary")),
        name="in_proj",
    )(x, norm1_g, mod, lbl, w_in_bf)


def _reads(i, j, reverse):
    return i <= j if reverse else i >= j


def _block_edges(b_ref, reverse):
    nb = HGRN_CHUNK // HGRN_BLOCK
    zero = jnp.zeros((1, HGRN_WIDTH), F32)
    out = []
    for j in range(nb):
        if reverse:
            r = (j + 1) * HGRN_BLOCK
            out.append(zero if j == nb - 1 else b_ref[r:r + 1, :])
        else:
            r = j * HGRN_BLOCK
            out.append(zero if j == 0 else b_ref[r - 1:r, :])
    return out


def _seen(reverse):
    t = HGRN_CHUNK
    row = lax.broadcasted_iota(I32, (t, t), 0)
    col = lax.broadcasted_iota(I32, (t, t), 1)
    return (col >= row) if reverse else (col <= row)


def _hgrn_decay(g_ref, b_ref, reverse):
    g = g_ref[0]
    hi = g.astype(BF16)
    lo = (g - hi.astype(F32)).astype(BF16)
    u01 = _one_zero(_seen(reverse))
    b_ref[...] = _dot(u01, hi) + _dot(u01, lo)


def _chunk_end(b_ref, reverse):
    return b_ref[0:1, :] if reverse else b_ref[HGRN_CHUNK - 1:HGRN_CHUNK, :]


def _decay_columns(b_end):
    e_end = jnp.exp(b_end)
    rows = [e_end[:, h * HEAD_DIM:(h + 1) * HEAD_DIM] for h in range(N_HEADS)]
    return jnp.concatenate(rows + [jnp.zeros((HEAD_DIM - N_HEADS, HEAD_DIM), F32)], axis=0).T


def _hgrn_fast(q_ref, v_ref, k_ref, b_ref, o_ref, s_ref, reverse):
    t, bl = HGRN_CHUNK, HGRN_BLOCK
    nb = t // bl
    seen = _seen(reverse)
    edges = _block_edges(b_ref, reverse)
    gain = {(i, j): jnp.exp(edges[i] - edges[j])
            for i in range(nb) for j in range(nb) if i != j and _reads(i, j, reverse)}
    b_end = _chunk_end(b_ref, reverse)
    e_cols = _decay_columns(b_end)
    zblk = jnp.zeros((bl, HEAD_DIM), BF16)
    full = nb - 1 if reverse else 0
    heads = [slice(h * HEAD_DIM, (h + 1) * HEAD_DIM) for h in range(N_HEADS)]

    scores, q_full, k_state = [], [], []
    for hs in heads:
        binc = b_ref[:, hs]
        q = q_ref[0, :, hs].astype(F32)
        k = k_ref[0, :, hs].astype(F32)
        q_own, k_own = [], []
        for i in range(nb):
            bs = slice(i * bl, (i + 1) * bl)
            q_own.append(q[bs] * jnp.exp(binc[bs] - edges[i][:, hs]))
            k_own.append((k[bs] * jnp.exp(edges[i][:, hs] - binc[bs])).astype(BF16))
        q_slabs = []
        for j in range(nb):
            parts = []
            for i in range(nb):
                if not _reads(i, j, reverse):
                    parts.append(zblk)
                elif i == j:
                    parts.append(q_own[i].astype(BF16))
                else:
                    parts.append((q_own[i] * gain[i, j][:, hs]).astype(BF16))
            q_slabs.append(jnp.concatenate(parts, axis=0))
        k_slabs = [jnp.concatenate([k_own[i] if i == j else zblk for i in range(nb)], axis=0)
                   for j in range(nb)]
        near = slice(nb // 2, nb) if reverse else slice(0, nb // 2)
        near_rows = slice(t // 2, t) if reverse else slice(0, t // 2)
        far_rows = slice(0, t // 2) if reverse else slice(t // 2, t)
        s_near = _dot_nt(jnp.concatenate([s[near_rows] for s in q_slabs[near]], axis=1),
                         jnp.concatenate(k_slabs[near], axis=1))
        s_far = _dot_nt(jnp.concatenate([s[far_rows] for s in q_slabs], axis=1),
                        jnp.concatenate(k_slabs, axis=1))
        scores.append(jnp.concatenate([s_far, s_near] if reverse else [s_near, s_far], axis=0))
        q_full.append(q_slabs[full])
        k_state.append((k * jnp.exp(b_end[:, hs] - binc)).astype(BF16))

    for h, hs in enumerate(heads):
        p = jnp.where(seen, scores[h], 0.0).astype(BF16)
        lhs = jnp.concatenate([p, q_full[h]], axis=1)
        rhs = jnp.concatenate([v_ref[0, :, hs], s_ref[h].astype(BF16)], axis=0)
        o_ref[0, :, hs] = _dot(lhs, rhs)

    for h, hs in enumerate(heads):
        s_ref[h] = s_ref[h] * e_cols[:, h:h + 1] + _dot_tn(k_state[h], v_ref[0, :, hs])


def _hgrn_slow(q_ref, v_ref, k_ref, b_ref, o_ref, s_ref, oi_ref, kk_ref, vv_ref, reverse):
    t = HGRN_CHUNK
    q = q_ref[0].astype(F32)
    binc = b_ref[...]
    kk_ref[...] = k_ref[0].astype(F32)
    vv_ref[...] = v_ref[0].astype(F32)
    oi_ref[...] = jnp.zeros((t, HGRN_WIDTH), F32)
    trow = lax.broadcasted_iota(I32, (t, HGRN_WIDTH), 0)
    heads = [slice(h * HEAD_DIM, (h + 1) * HEAD_DIM) for h in range(N_HEADS)]

    def body(s, carry):
        reads = (trow <= s) if reverse else (trow >= s)
        w = jnp.exp(jnp.where(reads, binc - b_ref[pl.ds(s, 1), :], 0.0))
        prod = jnp.where(reads, q * kk_ref[pl.ds(s, 1), :] * w, 0.0)
        vs = vv_ref[pl.ds(s, 1), :]
        for hs in heads:
            oi_ref[:, hs] += jnp.sum(prod[:, hs], axis=1, keepdims=True) * vs[:, hs]
        return carry

    lax.fori_loop(0, t, body, 0)
    b_end = _chunk_end(b_ref, reverse)
    e_cols = _decay_columns(b_end)
    q_full = (q * jnp.exp(binc)).astype(BF16)
    k_state = (kk_ref[...] * jnp.exp(b_end - binc)).astype(BF16)
    for h, hs in enumerate(heads):
        o_ref[0, :, hs] = oi_ref[:, hs] + _dot(q_full[:, hs], s_ref[h].astype(BF16))
        s_ref[h] = s_ref[h] * e_cols[:, h:h + 1] + _dot_tn(k_state[:, hs], v_ref[0, :, hs])


def _hgrn_kernel(flag_ref, qf_ref, vf_ref, kf_ref, gf_ref, qb_ref, vb_ref, kb_ref, gb_ref, s0f_ref, s0b_ref,
                 of_ref, ob_ref, sf_ref, sb_ref, bf_ref, bb_ref, oi_ref, kk_ref, vv_ref):
    b = pl.program_id(0)
    j = pl.program_id(1)
    n = pl.num_programs(1)

    @pl.when(j == 0)
    def _():
        sf_ref[...] = s0f_ref[0]
        sb_ref[...] = s0b_ref[0]

    safe = (flag_ref[(b * n + j) * 2] + flag_ref[(b * n + n - 1 - j) * 2 + 1]) == 2

    @pl.when(safe)
    def _():
        _hgrn_decay(gf_ref, bf_ref, False)
        _hgrn_decay(gb_ref, bb_ref, True)
        _hgrn_fast(qf_ref, vf_ref, kf_ref, bf_ref, of_ref, sf_ref, False)
        _hgrn_fast(qb_ref, vb_ref, kb_ref, bb_ref, ob_ref, sb_ref, True)

    @pl.when(jnp.logical_not(safe))
    def _():
        _hgrn_decay(gf_ref, bf_ref, False)
        _hgrn_decay(gb_ref, bb_ref, True)
        _hgrn_slow(qf_ref, vf_ref, kf_ref, bf_ref, of_ref, sf_ref, oi_ref, kk_ref, vv_ref, False)
        _hgrn_slow(qb_ref, vb_ref, kb_ref, bb_ref, ob_ref, sb_ref, oi_ref, kk_ref, vv_ref, True)


def _hgrn(flags, q, v, kf, gf, kb, gb, s0f, s0b):
    b, l, w = q.shape
    t = HGRN_CHUNK
    n = l // t
    fwd = pl.BlockSpec((1, t, w), lambda i, j, f: (i, j, 0))
    bwd = pl.BlockSpec((1, t, w), lambda i, j, f: (i, n - 1 - j, 0))
    sspec = pl.BlockSpec((1, N_HEADS, HEAD_DIM, HEAD_DIM), lambda i, j, f: (i, 0, 0, 0))
    out = jax.ShapeDtypeStruct((b, l, w), F32)
    state = pltpu.VMEM((N_HEADS, HEAD_DIM, HEAD_DIM), F32)
    rows = pltpu.VMEM((t, w), F32)
    grid_spec = pltpu.PrefetchScalarGridSpec(
        num_scalar_prefetch=1,
        grid=(b, n),
        in_specs=[fwd, fwd, fwd, fwd, bwd, bwd, bwd, bwd, sspec, sspec],
        out_specs=[fwd, bwd],
        scratch_shapes=[state, state, rows, rows, rows, rows, rows],
    )
    return pl.pallas_call(
        _hgrn_kernel,
        grid_spec=grid_spec,
        out_shape=[out, out],
        compiler_params=_params(("arbitrary", "arbitrary")),
        name="hgrn",
    )(flags, q, v, kf, gf, q, v, kb, gb, s0f, s0b)


def _pool_col_matrices():
    mats = np.zeros((len(POOL_WINDOWS), POOL_TILE, POOL_TILE), np.float32)
    for j, w in enumerate(POOL_WINDOWS):
        for t in range(POOL_TILE):
            r, c = divmod(t, GRID_W)
            lo, hi = max(c - w // 2, 0), min(c + w // 2 - 1, GRID_W - 1)
            mats[j, t, r * GRID_W + lo:r * GRID_W + hi + 1] = 1.0
    return jnp.asarray(mats, BF16)


def _window_len(pos, half, n):
    return jnp.minimum(pos + half - 1, n - 1) + 1 - jnp.maximum(pos - half, 0)


def _pool_kernel(p_ref, a_ref, wp_ref, ps_ref, o_ref, pad_ref, inr_ref, inc_ref, *, rows):
    j = pl.program_id(1)
    half = jnp.left_shift(1, j)
    gw = GRID_W
    halo = POOL_HALO * gw
    l = rows * gw
    tile_rows = POOL_TILE // gw
    edge = POOL_HALO // 2
    for buf in range(2):
        pad_ref[buf, 0:halo, :] = jnp.zeros((halo, POOL_GROUP), F32)
        pad_ref[buf, halo + l:2 * halo + l, :] = jnp.zeros((halo, POOL_GROUP), F32)
    pad_ref[0, halo:halo + l, :] = p_ref[0]

    def level(src, dst, back, fwd):
        def body(g, carry):
            for u in range(8):
                t = pl.multiple_of((edge + g * 8 + u) * gw, gw)
                pad_ref[dst, pl.ds(t, gw), :] = (pad_ref[src, pl.ds(t - back * gw, gw), :]
                                                 + pad_ref[src, pl.ds(t + fwd * gw, gw), :])
            return carry

        lax.fori_loop(0, (rows + 2 * (POOL_HALO - edge)) // 8, body, 0)

    level(0, 1, 1, 0)

    @pl.when(j >= 1)
    def _():
        level(1, 0, 1, 1)

    @pl.when(j >= 2)
    def _():
        level(0, 1, 2, 2)

    @pl.when(j >= 3)
    def _():
        level(1, 0, 4, 4)

    summed = 1 - jnp.bitwise_and(j, 1)
    r = lax.broadcasted_iota(I32, inr_ref.shape, 0)
    inr_ref[...] = 1.0 / _window_len(r, half, rows).astype(F32)
    c = jnp.bitwise_and(lax.broadcasted_iota(I32, inc_ref.shape, 0), gw - 1)
    inc_ref[...] = 1.0 / _window_len(c, half, gw).astype(F32)
    a = a_ref[0]
    wp = wp_ref[0]
    scale = ps_ref[0]

    def tile(i):
        t0 = pl.multiple_of(i * POOL_TILE, POOL_TILE)
        rsum = pad_ref[summed, pl.ds(halo + t0, POOL_TILE), :]
        hi = rsum.astype(BF16)
        lo = (rsum - hi.astype(F32)).astype(BF16)
        both = _dot(a, jnp.concatenate([hi, lo], axis=1))
        total = both[:, :POOL_GROUP] + both[:, POOL_GROUP:]
        by_row = [total[rr * gw:(rr + 1) * gw] * inr_ref[pl.ds(i * tile_rows + rr, 1), :]
                  for rr in range(tile_rows)]
        resid = jnp.concatenate(by_row, axis=0) * inc_ref[...] - p_ref[0, pl.ds(t0, POOL_TILE), :]
        o_ref[0, pl.ds(t0, POOL_TILE), :] = (_dot(resid.astype(BF16), wp) * scale).astype(BF16)

    def step(i, carry):
        for u in range(POOL_UNROLL):
            tile(POOL_UNROLL * i + u)
        return carry

    lax.fori_loop(0, l // (POOL_UNROLL * POOL_TILE), step, 0)


def _pool(p, w_pool_bf, pool_scale):
    b, l, _ = p.shape
    rows = l // GRID_W
    ng = len(POOL_WINDOWS)
    grp = pl.BlockSpec((1, l, POOL_GROUP), lambda i, j: (i, 0, j))
    return pl.pallas_call(
        functools.partial(_pool_kernel, rows=rows),
        grid=(b, ng),
        in_specs=[grp,
                  pl.BlockSpec((1, POOL_TILE, POOL_TILE), lambda i, j: (j, 0, 0)),
                  pl.BlockSpec((1, POOL_GROUP, POOL_GROUP), lambda i, j: (j, 0, 0)),
                  pl.BlockSpec((1, 1, POOL_GROUP), lambda i, j: (j, 0, 0))],
        out_specs=grp,
        out_shape=jax.ShapeDtypeStruct((b, l, POOL_WIDTH), BF16),
        scratch_shapes=[pltpu.VMEM((2, (rows + 2 * POOL_HALO) * GRID_W, POOL_GROUP), F32),
                        pltpu.VMEM((rows, POOL_GROUP), F32),
                        pltpu.VMEM((POOL_TILE, POOL_GROUP), F32)],
        compiler_params=_params(("arbitrary", "arbitrary")),
        name="pool",
    )(p, _pool_col_matrices(), w_pool_bf, pool_scale.reshape(ng, 1, POOL_GROUP))


def _merge_kernel(x_ref, of_ref, ob_ref, og_ref, pl_ref, ga_ref, gb_ref, mod_ref, hg_ref, n2_ref,
                  wa_ref, wb_ref, wo_ref, wr_ref, nh_ref, vx_ref, pr_ref):
    b = pl.program_id(0)
    m = mod_ref[pl.ds(b, 1), :]
    g1 = m[:, 2 * D_MODEL:3 * D_MODEL]
    sh2 = m[:, 3 * D_MODEL:4 * D_MODEL]
    sc2 = m[:, 4 * D_MODEL:5 * D_MODEL]
    hg = hg_ref[...]
    for r0 in range(0, x_ref.shape[1], MERGE_ROWS):
        rs = slice(r0, r0 + MERGE_ROWS)
        gated = []
        for h in range(N_HEADS):
            hs = slice(h * HEAD_DIM, (h + 1) * HEAD_DIM)
            o = of_ref[0, rs, hs] + ob_ref[0, rs, hs]
            gated.append((_rms(o, hg) * og_ref[0, rs, hs].astype(F32)).astype(BF16))
        ya = _dot(jnp.concatenate(gated, axis=1), wa_ref[...])
        yb = _dot(pl_ref[0, rs, :], wb_ref[...])
        y = ga_ref[0, rs, :].astype(F32) * ya + gb_ref[0, rs, :].astype(F32) * yb
        nh = x_ref[0, rs, :] + g1 * _dot(y.astype(BF16), wo_ref[...])
        nh_ref[0, rs, :] = nh
        vx = (_rms(nh, n2_ref[...]) * (1.0 + sc2) + sh2).astype(BF16)
        vx_ref[0, rs, :] = vx
        logits = _dot_nt(wr_ref[...], vx)
        ex = jnp.exp(logits - jnp.max(logits, axis=0, keepdims=True))
        pr_ref[0, :, rs] = ex / jnp.sum(ex, axis=0, keepdims=True)


def _merge(x, o_f, o_b, og, pooled, ga, gb, mod, hgrn_g, norm2_g, w_a_bf, w_b_bf, w_out_bf, w_router_t):
    b, l, d = x.shape
    tm = PROJ_TILE
    tok = lambda w: pl.BlockSpec((1, tm, w), lambda i, j: (i, j, 0))
    const = lambda a: pl.BlockSpec(a.shape, lambda i, j: (0,) * a.ndim)
    return pl.pallas_call(
        _merge_kernel,
        grid=(b, l // tm),
        in_specs=[tok(d), tok(d), tok(d), tok(d), tok(POOL_WIDTH), tok(d), tok(d),
                  const(mod), const(hgrn_g), const(norm2_g),
                  _resident(w_a_bf.shape, lambda i, j: (0, 0)),
                  _resident(w_b_bf.shape, lambda i, j: (0, 0)),
                  _resident(w_out_bf.shape, lambda i, j: (0, 0)),
                  const(w_router_t)],
        out_specs=[tok(d), tok(d), pl.BlockSpec((1, N_EXPERTS, tm), lambda i, j: (i, 0, j))],
        out_shape=[jax.ShapeDtypeStruct((b, l, d), F32),
                   jax.ShapeDtypeStruct((b, l, d), BF16),
                   jax.ShapeDtypeStruct((b, N_EXPERTS, l), F32)],
        compiler_params=_params(("arbitrary", "arbitrary")),
        name="merge",
    )(x, o_f, o_b, og, pooled, ga, gb, mod, hgrn_g, norm2_g, w_a_bf, w_b_bf, w_out_bf, w_router_t)


def _route_kernel(p_ref, code_ref, start_ref, *, cap):
    l = p_ref.shape[2]
    tm = TOKEN_TILE
    bits = lax.bitcast_convert_type(p_ref[0], I32)
    capf = jnp.float32(cap)

    def count_ge(cand):
        return jnp.sum(jnp.where(bits >= cand, 1.0, 0.0), axis=1, keepdims=True)

    def search(i, prefix):
        cand = jnp.bitwise_or(prefix, jnp.left_shift(1, 30 - i))
        return jnp.where(count_ge(cand) >= capf, cand, prefix)

    tau = lax.fori_loop(0, 31, search, jnp.zeros((N_EXPERTS, 1), I32))
    need = capf - jnp.sum(jnp.where(bits > tau, 1.0, 0.0), axis=1, keepdims=True)
    before = _one_zero(lax.broadcasted_iota(I32, (tm, tm), 0) < lax.broadcasted_iota(I32, (tm, tm), 1))
    lane = lax.broadcasted_iota(I32, (N_EXPERTS, 128), 1)

    def tile(k, carry):
        n_gt, n_eq, starts = carry
        sl = pl.ds(pl.multiple_of(k * tm, tm), tm)
        bk = lax.bitcast_convert_type(p_ref[0, :, sl], I32)
        gt = bk > tau
        eq = bk == tau
        gt01 = _one_zero(gt)
        eq01 = _one_zero(eq)
        eq_before = n_eq + _dot(eq01, before)
        chosen = jnp.logical_or(gt, jnp.logical_and(eq, eq_before < need))
        rank = n_gt + _dot(gt01, before) + jnp.minimum(eq_before, need)
        code_ref[0, :, sl] = jnp.where(chosen, rank, -1.0).astype(I32)
        starts = jnp.where(lane == k, (n_gt + jnp.minimum(n_eq, need)).astype(I32), starts)
        n_gt = n_gt + jnp.sum(gt01.astype(F32), axis=1, keepdims=True)
        n_eq = n_eq + jnp.sum(eq01.astype(F32), axis=1, keepdims=True)
        return n_gt, n_eq, starts

    zero = jnp.zeros((N_EXPERTS, 1), F32)
    _, _, starts = lax.fori_loop(0, l // tm, tile, (zero, zero, jnp.zeros((N_EXPERTS, 128), I32)))
    start_ref[0] = jnp.where(lane == l // tm, cap, starts)


def _route(probs_t, cap):
    b, e, l = probs_t.shape
    assert l // TOKEN_TILE < 128
    return pl.pallas_call(
        functools.partial(_route_kernel, cap=cap),
        grid=(b,),
        in_specs=[pl.BlockSpec((1, e, l), lambda i: (i, 0, 0))],
        out_specs=[pl.BlockSpec((1, e, l), lambda i: (i, 0, 0)),
                   pl.BlockSpec((1, e, 128), lambda i: (i, 0, 0))],
        out_shape=[jax.ShapeDtypeStruct((b, e, l), I32), jax.ShapeDtypeStruct((b, e, 128), I32)],
        compiler_params=_params(("arbitrary",)),
        name="route",
    )(probs_t)


def _floor8(n):
    return jnp.left_shift(jnp.right_shift(n, 3), 3)


def _dispatch_kernel(start_ref, vx_ref, code_ref, xg_hbm, g_ref, stage_ref, carry_ref, sem,
                     *, cap, n_tiles, n_steps):
    b = pl.program_id(0)
    k = pl.program_id(1)
    step = b * n_tiles + k
    w = SLOT_WINDOW
    tm = TOKEN_TILE
    ne = N_EXPERTS

    @pl.when(k == 0)
    def _():
        carry_ref[...] = jnp.zeros(carry_ref.shape, F32)

    base, last, off = [], [], []
    n_pass = jnp.int32(1)
    for e in range(ne):
        i0 = (b * ne + e) * 128 + k
        bs = _floor8(start_ref[i0])
        tail = _floor8(start_ref[i0 + 1]) - bs
        base.append(bs)
        last.append(jnp.right_shift(tail, SLOT_SHIFT))
        off.append(jnp.bitwise_and(tail, w - 1))
        n_pass = jnp.maximum(n_pass, last[e] + 1)

    def copies(c):
        return [pltpu.make_async_copy(
            stage_ref.at[pl.ds(e * w, w), :],
            xg_hbm.at[b, e, pl.ds(pl.multiple_of(jnp.minimum(base[e] + c * w, cap), 8), w), :],
            sem.at[e]) for e in range(ne)]

    slot = lax.broadcasted_iota(I32, (w, tm), 0)

    def one_pass(c, carry):
        hits = [_one_zero(code_ref[0, e:e + 1, :] == (slot + (base[e] + c * w))) for e in range(ne)]
        g_ref[...] = _dot(jnp.concatenate(hits, axis=0), vx_ref[0])

        @pl.when(c == 0)
        def _():
            for e in range(ne):
                g_ref[e * w:e * w + 8, :] += carry_ref[e]

        for e in range(ne):
            @pl.when(c == last[e])
            def _(e=e):
                carry_ref[e] = g_ref[pl.ds(pl.multiple_of(e * w + off[e], 8), 8), :]

        @pl.when(jnp.logical_or(step > 0, c > 0))
        def _():
            for cp in copies(c):
                cp.wait()

        stage_ref[...] = g_ref[...].astype(BF16)
        for cp in copies(c):
            cp.start()
        return carry

    lax.fori_loop(0, n_pass, one_pass, 0)

    @pl.when(k == n_tiles - 1)
    def _():
        for cp in copies(0):
            cp.wait()
        stage_ref[...] = jnp.zeros(stage_ref.shape, BF16)
        for e in range(ne):
            pltpu.make_async_copy(stage_ref.at[pl.ds(e * w, w), :], xg_hbm.at[b, e, pl.ds(cap, w), :],
                                  sem.at[e]).start()

    @pl.when(step == n_steps - 1)
    def _():
        for cp in copies(0):
            cp.wait()


def _dispatch(starts_flat, vx, code, cap):
    b, l, d = vx.shape
    tm = TOKEN_TILE
    n_tiles = l // tm
    grid_spec = pltpu.PrefetchScalarGridSpec(
        num_scalar_prefetch=1,
        grid=(b, n_tiles),
        in_specs=[pl.BlockSpec((1, tm, d), lambda i, j, s: (i, j, 0)),
                  pl.BlockSpec((1, N_EXPERTS, tm), lambda i, j, s: (i, 0, j))],
        out_specs=pl.BlockSpec(memory_space=pl.ANY),
        scratch_shapes=[pltpu.VMEM((N_EXPERTS * SLOT_WINDOW, d), F32),
                        pltpu.VMEM((N_EXPERTS * SLOT_WINDOW, d), BF16),
                        pltpu.VMEM((N_EXPERTS, 8, d), F32),
                        pltpu.SemaphoreType.DMA((N_EXPERTS,))],
    )
    return pl.pallas_call(
        functools.partial(_dispatch_kernel, cap=cap, n_tiles=n_tiles, n_steps=b * n_tiles),
        grid_spec=grid_spec,
        out_shape=jax.ShapeDtypeStruct((b, N_EXPERTS, cap + SLOT_WINDOW, d), BF16),
        compiler_params=_params(("arbitrary", "arbitrary")),
        name="dispatch",
    )(starts_flat, vx, code)


def _moe_kernel(xg_ref, wg_ref, wu_ref, wd_ref, y_ref, wgb_ref, wub_ref, wdb_ref, *, cap):
    @pl.when(pl.program_id(1) == 0)
    def _():
        wgb_ref[...] = wg_ref[0].astype(BF16)
        wub_ref[...] = wu_ref[0].astype(BF16)
        wdb_ref[...] = wd_ref[0].astype(BF16)

    fr = min(FFN_ROWS, cap)
    for r0 in range(0, cap, fr):
        xg = xg_ref[0, 0, r0:r0 + fr, :]
        hid = (_silu(_dot(xg, wgb_ref[...])) * _dot(xg, wub_ref[...])).astype(BF16)
        y_ref[0, 0, r0:r0 + fr, :] = _dot(hid, wdb_ref[...]).astype(BF16)


def _moe(xg, w_gate, w_up, w_down, cap):
    b, ne, _, d = xg.shape
    ff = w_gate.shape[2]
    rows = pl.BlockSpec((1, 1, cap, d), lambda e, i: (i, e, 0, 0))
    return pl.pallas_call(
        functools.partial(_moe_kernel, cap=cap),
        grid=(ne, b),
        in_specs=[rows,
                  pl.BlockSpec((1, d, ff), lambda e, i: (e, 0, 0)),
                  pl.BlockSpec((1, d, ff), lambda e, i: (e, 0, 0)),
                  pl.BlockSpec((1, ff, d), lambda e, i: (e, 0, 0))],
        out_specs=rows,
        out_shape=jax.ShapeDtypeStruct((b, ne, cap, d), BF16),
        scratch_shapes=[pltpu.VMEM((d, ff), BF16), pltpu.VMEM((d, ff), BF16), pltpu.VMEM((ff, d), BF16)],
        compiler_params=_params(("arbitrary", "arbitrary")),
        name="moe",
    )(xg, w_gate, w_up, w_down)


def _combine_kernel(start_ref, nh_ref, code_ref, p_ref, mod_ref, fg_ref, y_hbm, o_ref, ybuf, sem,
                    *, cap, n_tiles, n_steps):
    b = pl.program_id(0)
    k = pl.program_id(1)
    step = b * n_tiles + k
    tm = TOKEN_TILE
    w = SLOT_WINDOW

    def window(bb, kk, e, c):
        st = start_ref[(bb * N_EXPERTS + e) * 128 + kk]
        lo = jnp.left_shift(jnp.right_shift(st, 3), 3) + c * w
        return lo, pl.multiple_of(jnp.minimum(lo, cap - w), 8)

    def copies(bb, kk, buf):
        out = []
        for e in range(N_EXPERTS):
            _, src = window(bb, kk, e, 0)
            out.append(pltpu.make_async_copy(y_hbm.at[bb, e, pl.ds(src, w), :], ybuf.at[buf, e], sem.at[buf, e]))
        return out

    @pl.when(step == 0)
    def _():
        for cp in copies(b, k, 0):
            cp.start()

    cur = lax.rem(step, 2)

    @pl.when(step + 1 < n_steps)
    def _():
        nk = jnp.where(k + 1 == n_tiles, 0, k + 1)
        nb = jnp.where(k + 1 == n_tiles, b + 1, b)
        for cp in copies(nb, nk, 1 - cur):
            cp.start()

    for cp in copies(b, k, cur):
        cp.wait()

    slot = lax.broadcasted_iota(I32, (w, tm), 0)
    p = p_ref[0]
    p_hi = p.astype(BF16).astype(F32)
    p_lo = p - p_hi

    def scatter(c, buf_rows):
        hi, lo_terms = [], []
        for e in range(N_EXPERTS):
            lo, src = window(b, k, e, c)
            codes = code_ref[0, e:e + 1, :]
            hit = jnp.logical_and(codes == (slot + src), jnp.logical_and(codes >= lo, codes < lo + w))
            hi.append(jnp.where(hit, p_hi[e:e + 1, :], 0.0).astype(BF16))
            lo_terms.append(jnp.where(hit, p_lo[e:e + 1, :], 0.0).astype(BF16))
        return (_dot_tn(jnp.concatenate(hi, axis=0), buf_rows)
                + _dot_tn(jnp.concatenate(lo_terms, axis=0), buf_rows))

    acc = scatter(0, ybuf[cur].reshape(N_EXPERTS * w, D_MODEL))

    most = jnp.int32(0)
    for e in range(N_EXPERTS):
        i0 = (b * N_EXPERTS + e) * 128 + k
        st = start_ref[i0]
        span = st - jnp.left_shift(jnp.right_shift(st, 3), 3) + start_ref[i0 + 1] - st
        most = jnp.maximum(most, jnp.right_shift(span + w - 1, SLOT_SHIFT))

    def extra(c, acc):
        cps = []
        for e in range(N_EXPERTS):
            _, src = window(b, k, e, c)
            cps.append(pltpu.make_async_copy(y_hbm.at[b, e, pl.ds(src, w), :], ybuf.at[cur, e], sem.at[cur, e]))
        for cp in cps:
            cp.start()
        for cp in cps:
            cp.wait()
        return acc + scatter(c, ybuf[cur].reshape(N_EXPERTS * w, D_MODEL))

    acc = lax.fori_loop(1, most, extra, acc)
    g2 = mod_ref[pl.ds(b, 1), 5 * D_MODEL:6 * D_MODEL]
    o_ref[0] = _rms(nh_ref[0] + g2 * acc, fg_ref[...])


def _combine(starts_flat, new_hx, code, probs_t, mod, final_g, y, cap):
    b, l, d = new_hx.shape
    tm = TOKEN_TILE
    n_tiles = l // tm
    grid_spec = pltpu.PrefetchScalarGridSpec(
        num_scalar_prefetch=1,
        grid=(b, n_tiles),
        in_specs=[pl.BlockSpec((1, tm, d), lambda i, j, s: (i, j, 0)),
                  pl.BlockSpec((1, N_EXPERTS, tm), lambda i, j, s: (i, 0, j)),
                  pl.BlockSpec((1, N_EXPERTS, tm), lambda i, j, s: (i, 0, j)),
                  pl.BlockSpec(mod.shape, lambda i, j, s: (0, 0)),
                  pl.BlockSpec((1, d), lambda i, j, s: (0, 0)),
                  pl.BlockSpec(memory_space=pl.ANY)],
        out_specs=pl.BlockSpec((1, tm, d), lambda i, j, s: (i, j, 0)),
        scratch_shapes=[pltpu.VMEM((2, N_EXPERTS, SLOT_WINDOW, d), BF16),
                        pltpu.SemaphoreType.DMA((2, N_EXPERTS))],
    )
    return pl.pallas_call(
        functools.partial(_combine_kernel, cap=cap, n_tiles=n_tiles, n_steps=b * n_tiles),
        grid_spec=grid_spec,
        out_shape=jax.ShapeDtypeStruct((b, l, d), F32),
        compiler_params=_params(("arbitrary", "arbitrary")),
        name="combine",
    )(starts_flat, new_hx, code, probs_t, mod, final_g, y)


def _layer(x, c, ctx, c_ctx, lb_logits, w_mod, b_mod, norm1_g, w_in, hgrn_norm_g, w_a, w_pool, pool_scale,
           w_b, w_out, norm2_g, w_router, w_e_gate, w_e_up, w_e_down, final_g):
    b, l, d = x.shape
    cap = CAPACITY_FACTOR * l // N_EXPERTS
    assert b < MOD_ROWS and l % HGRN_CHUNK == 0 and l % PROJ_TILE == 0 and l % TOKEN_TILE == 0
    assert cap % 8 == 0 and cap >= SLOT_WINDOW and l % (POOL_UNROLL * POOL_TILE) == 0
    cc = jnp.zeros((MOD_ROWS, d), F32).at[:b].set(c).at[b].set(c_ctx)
    assert lb_logits.shape == (2, 2, HGRN_WIDTH)
    lbl = lb_logits.reshape(4, HGRN_WIDTH)
    row = lambda a: a.reshape(1, -1)
    w_in_bf = w_in.astype(BF16)

    mod = _adaln(cc, w_mod, b_mod)
    s0f, s0b = _ctx_states(ctx, row(norm1_g), mod, lbl, w_in_bf)
    q, v, kf, gf, kb, gb, og, p, ga, gbm, safe = _in_proj(x, row(norm1_g), mod, lbl, w_in_bf)
    per_tile = TOKEN_TILE // HGRN_CHUNK
    safe = safe[:, :, :2 * per_tile, 0].reshape(b, -1, 2, per_tile)
    flags = safe.transpose(0, 1, 3, 2).astype(I32).reshape(-1)
    o_f, o_b = _hgrn(flags, q, v, kf, gf, kb, gb, s0f, s0b)
    pooled = _pool(p, w_pool.astype(BF16), pool_scale)
    new_hx, vx, probs_t = _merge(x, o_f, o_b, og, pooled, ga, gbm, mod, row(hgrn_norm_g), row(norm2_g),
                                 w_a.astype(BF16), w_b.astype(BF16), w_out.astype(BF16),
                                 w_router.T.astype(BF16))
    code, starts = _route(probs_t, cap)
    starts_flat = starts.reshape(-1)
    xg = _dispatch(starts_flat, vx, code, cap)
    y = _moe(xg, w_e_gate, w_e_up, w_e_down, cap)
    return _combine(starts_flat, new_hx, code, probs_t, mod, row(final_g), y, cap)


def kernel(x, c, ctx, c_ctx, lb_logits, w_mod, b_mod, norm1_g, w_in, hgrn_norm_g, w_a, w_pool, pool_scale,
           w_b, w_out, norm2_g, w_router, w_e_gate, w_e_up, w_e_down, final_g):
    assert w_mod.shape[0] == 1, "single-layer trunk: the context stream only seeds the latent recurrence"
    return _layer(x, c, ctx, c_ctx, lb_logits, w_mod[0], b_mod[0], norm1_g[0], w_in[0], hgrn_norm_g[0], w_a[0],
                  w_pool[0], pool_scale[0], w_b[0], w_out[0], norm2_g[0], w_router[0], w_e_gate[0],
                  w_e_up[0], w_e_down[0], final_g)
```

```python
import functools

import numpy as np
import jax
import jax.numpy as jnp
from jax import lax
from jax.experimental import pallas as pl
from jax.experimental.pallas import tpu as pltpu

F32 = jnp.float32
BF16 = jnp.bfloat16
I32 = jnp.int32

D_MODEL = 1024
N_HEADS = 8
HEAD_DIM = 128
HGRN_WIDTH = N_HEADS * HEAD_DIM
GRID_W = 64
POOL_WINDOWS = (2, 4, 8, 16)
POOL_GROUP = 128
POOL_WIDTH = POOL_GROUP * len(POOL_WINDOWS)
N_EXPERTS = 16
CAPACITY_FACTOR = 2
EXPERT_FF = 1024
EPS = 1e-6

Q_OFF = 0
I_OFF = Q_OFF + HGRN_WIDTH
FF_OFF = I_OFF + HGRN_WIDTH
FB_OFF = FF_OFF + HGRN_WIDTH
OG_OFF = FB_OFF + HGRN_WIDTH
P_OFF = OG_OFF + HGRN_WIDTH
GA_OFF = P_OFF + POOL_WIDTH
GB_OFF = GA_OFF + D_MODEL
IN_COLS = GB_OFF + D_MODEL

MOD_ROWS = 16
PROJ_TILE = 512
MERGE_ROWS = 512
TOKEN_TILE = 256
HGRN_CHUNK = 128
HGRN_BLOCK = 32
HGRN_SAFE_DECAY = 80.0
POOL_HALO = 16
POOL_TILE = 256
POOL_UNROLL = 4
SLOT_SHIFT = 6
SLOT_WINDOW = 1 << SLOT_SHIFT
FFN_ROWS = 256
VMEM_LIMIT = 56 * 1024 * 1024


def _dot(a, b):
    return jnp.dot(a, b, preferred_element_type=F32)


def _dot_nt(a, b):
    return lax.dot_general(a, b, (((1,), (1,)), ((), ())), preferred_element_type=F32)


def _dot_tn(a, b):
    return lax.dot_general(a, b, (((0,), (0,)), ((), ())), preferred_element_type=F32)


def _rms(x, g):
    ms = jnp.mean(x * x, axis=-1, keepdims=True)
    return x * lax.rsqrt(ms + EPS) * g


def _silu(z):
    return z * jax.nn.sigmoid(z)


def _lower_bound(a0, a1):
    m = jnp.maximum(a0, a1)
    e0 = jnp.exp(a0 - m)
    e1 = jnp.exp(a1 - m)
    return e0 / (e0 + e1)


def _split3(g):
    hi = g.astype(BF16)
    r = g - hi.astype(F32)
    mid = r.astype(BF16)
    lo = (r - mid.astype(F32)).astype(BF16)
    return hi, mid, lo


def _apply01(u01, g):
    hi, mid, lo = _split3(g)
    return _dot(u01, hi) + _dot(u01, mid) + _dot(u01, lo)


def _one_zero(mask):
    return jnp.where(mask, 1.0, 0.0).astype(BF16)


def _params(sem):
    return pltpu.CompilerParams(dimension_semantics=sem, vmem_limit_bytes=VMEM_LIMIT)


def _resident(shape, index_map):
    return pl.BlockSpec(shape, index_map, pipeline_mode=pl.Buffered(1))


def _adaln_kernel(c_ref, w_ref, b_ref, o_ref):
    c = c_ref[...]
    o_ref[...] = _dot(_silu(c).astype(BF16), w_ref[...].astype(BF16)) + b_ref[...]


def _adaln(cc, w_mod, b_mod):
    n = w_mod.shape[1]
    tn = 768
    return pl.pallas_call(
        _adaln_kernel,
        grid=(n // tn,),
        in_specs=[pl.BlockSpec((MOD_ROWS, D_MODEL), lambda j: (0, 0)),
                  pl.BlockSpec((D_MODEL, tn), lambda j: (0, j)),
                  pl.BlockSpec((1, tn), lambda j: (0, j))],
        out_specs=pl.BlockSpec((MOD_ROWS, tn), lambda j: (0, j)),
        out_shape=jax.ShapeDtypeStruct((MOD_ROWS, n), F32),
        compiler_params=_params(("arbitrary",)),
        name="adaln",
    )(cc, w_mod, b_mod.reshape(1, n))


def _ctx_kernel(ctx_ref, g1_ref, mod_ref, lbl_ref, wi_ref, wf_ref, wb_ref, sf_ref, sb_ref, *, ctx_row):
    x = ctx_ref[0]
    n = x.shape[0]
    sh = mod_ref[ctx_row:ctx_row + 1, 0:D_MODEL]
    sc = mod_ref[ctx_row:ctx_row + 1, D_MODEL:2 * D_MODEL]
    u = (_rms(x, g1_ref[...]) * (1.0 + sc) + sh).astype(BF16)
    v = _dot(u, wi_ref[...]).astype(BF16)
    zf = _dot(u, wf_ref[...])
    zb = _dot(u, wb_ref[...])
    lbf = _lower_bound(lbl_ref[0:1, :], lbl_ref[1:2, :])
    lbb = _lower_bound(lbl_ref[2:3, :], lbl_ref[3:4, :])
    ff = lbf + (1.0 - lbf) * jax.nn.sigmoid(zf)
    fb = lbb + (1.0 - lbb) * jax.nn.sigmoid(zb)
    r = lax.broadcasted_iota(I32, (n, n), 0)
    c = lax.broadcasted_iota(I32, (n, n), 1)
    ef = _apply01(_one_zero(c > r), jnp.log(ff))
    eb = _apply01(_one_zero(c < r), jnp.log(fb))
    kf = ((1.0 - ff) * jnp.exp(ef)).astype(BF16)
    kb = ((1.0 - fb) * jnp.exp(eb)).astype(BF16)
    for h in range(N_HEADS):
        hs = slice(h * HEAD_DIM, (h + 1) * HEAD_DIM)
        sf_ref[0, h] = _dot_tn(kf[:, hs], v[:, hs])
        sb_ref[0, h] = _dot_tn(kb[:, hs], v[:, hs])


def _ctx_states(ctx, norm1_g, mod, lbl, w_in_bf):
    b, lc, d = ctx.shape
    col = lambda k: pl.BlockSpec((d, HGRN_WIDTH), lambda i, k=k: (0, k))
    state = jax.ShapeDtypeStruct((b, N_HEADS, HEAD_DIM, HEAD_DIM), F32)
    sspec = pl.BlockSpec((1, N_HEADS, HEAD_DIM, HEAD_DIM), lambda i: (i, 0, 0, 0))
    return pl.pallas_call(
        functools.partial(_ctx_kernel, ctx_row=b),
        grid=(b,),
        in_specs=[pl.BlockSpec((1, lc, d), lambda i: (i, 0, 0)),
                  pl.BlockSpec((1, d), lambda i: (0, 0)),
                  pl.BlockSpec(mod.shape, lambda i: (0, 0)),
                  pl.BlockSpec(lbl.shape, lambda i: (0, 0)),
                  col(I_OFF // HGRN_WIDTH), col(FF_OFF // HGRN_WIDTH), col(FB_OFF // HGRN_WIDTH)],
        out_specs=[sspec, sspec],
        out_shape=[state, state],
        compiler_params=_params(("arbitrary",)),
        name="ctx_state",
    )(ctx, norm1_g, mod, lbl, w_in_bf, w_in_bf, w_in_bf)


def _chunks_safe(logf):
    out = []
    for c0 in range(0, logf.shape[0], HGRN_CHUNK):
        worst = None
        for lo in range(c0, c0 + HGRN_CHUNK, HGRN_BLOCK):
            dec = jnp.sum(logf[lo:lo + HGRN_BLOCK], axis=0, keepdims=True)
            worst = dec if worst is None else jnp.minimum(worst, dec)
        ok = jnp.min(worst, axis=1, keepdims=True) >= -HGRN_SAFE_DECAY
        out.append(jnp.broadcast_to(jnp.where(ok, 1.0, 0.0), (1, 128)))
    return out


def _inproj_kernel(x_ref, g1_ref, mod_ref, lbl_ref, w_ref,
                   q_ref, v_ref, kf_ref, gf_ref, kb_ref, gb_ref, og_ref, p_ref, ga_ref, gbm_ref, safe_ref):
    b = pl.program_id(0)
    x = x_ref[0]
    m = mod_ref[pl.ds(b, 1), :]
    sh = m[:, 0:D_MODEL]
    sc = m[:, D_MODEL:2 * D_MODEL]
    u = (_rms(x, g1_ref[...]) * (1.0 + sc) + sh).astype(BF16)

    z = _dot(u, w_ref[:, Q_OFF:I_OFF])
    q_ref[0] = _silu(z).astype(BF16)
    v_ref[0] = _dot(u, w_ref[:, I_OFF:FF_OFF]).astype(BF16)

    lbf = _lower_bound(lbl_ref[0:1, :], lbl_ref[1:2, :])
    f = lbf + (1.0 - lbf) * jax.nn.sigmoid(_dot(u, w_ref[:, FF_OFF:FB_OFF]))
    kf_ref[0] = (1.0 - f).astype(BF16)
    logf = jnp.log(f)
    gf_ref[0] = logf
    flags = _chunks_safe(logf)
    lbb = _lower_bound(lbl_ref[2:3, :], lbl_ref[3:4, :])
    f = lbb + (1.0 - lbb) * jax.nn.sigmoid(_dot(u, w_ref[:, FB_OFF:OG_OFF]))
    kb_ref[0] = (1.0 - f).astype(BF16)
    logf = jnp.log(f)
    gb_ref[0] = logf
    flags = flags + _chunks_safe(logf)
    safe_ref[0, 0] = jnp.concatenate(flags + [jnp.zeros((8 - len(flags), 128), F32)], axis=0)

    z = _dot(u, w_ref[:, OG_OFF:P_OFF])
    og_ref[0] = _silu(z).astype(BF16)
    p_ref[0] = _dot(u, w_ref[:, P_OFF:GA_OFF])
    ga_ref[0] = jax.nn.sigmoid(_dot(u, w_ref[:, GA_OFF:GB_OFF])).astype(BF16)
    gbm_ref[0] = jax.nn.sigmoid(_dot(u, w_ref[:, GB_OFF:IN_COLS])).astype(BF16)


def _in_proj(x, norm1_g, mod, lbl, w_in_bf):
    b, l, d = x.shape
    tm = TOKEN_TILE
    tok = lambda w: pl.BlockSpec((1, tm, w), lambda i, j: (i, j, 0))
    wide = lambda dt, w=HGRN_WIDTH: jax.ShapeDtypeStruct((b, l, w), dt)
    return pl.pallas_call(
        _inproj_kernel,
        grid=(b, l // tm),
        in_specs=[tok(d),
                  pl.BlockSpec((1, d), lambda i, j: (0, 0)),
                  pl.BlockSpec(mod.shape, lambda i, j: (0, 0)),
                  pl.BlockSpec(lbl.shape, lambda i, j: (0, 0)),
                  _resident((d, IN_COLS), lambda i, j: (0, 0))],
        out_specs=[tok(HGRN_WIDTH)] * 7 + [tok(POOL_WIDTH), tok(d), tok(d),
                                            pl.BlockSpec((1, 1, 8, 128), lambda i, j: (i, j, 0, 0))],
        out_shape=[wide(BF16), wide(BF16), wide(BF16), wide(F32), wide(BF16), wide(F32), wide(BF16),
                   wide(F32, POOL_WIDTH), wide(BF16, d), wide(BF16, d),
                   jax.ShapeDtypeStruct((b, l // tm, 8, 128), F32)],
        compiler_params=_params(("arbitrary", "arbitrary")),
        name="in_proj",
    )(x, norm1_g, mod, lbl, w_in_bf)


def _reads(i, j, reverse):
    return i <= j if reverse else i >= j


def _block_edges(b_ref, reverse):
    nb = HGRN_CHUNK // HGRN_BLOCK
    zero = jnp.zeros((1, HGRN_WIDTH), F32)
    out = []
    for j in range(nb):
        if reverse:
            r = (j + 1) * HGRN_BLOCK
            out.append(zero if j == nb - 1 else b_ref[r:r + 1, :])
        else:
            r = j * HGRN_BLOCK
            out.append(zero if j == 0 else b_ref[r - 1:r, :])
    return out


def _seen(reverse):
    t = HGRN_CHUNK
    row = lax.broadcasted_iota(I32, (t, t), 0)
    col = lax.broadcasted_iota(I32, (t, t), 1)
    return (col >= row) if reverse else (col <= row)


def _hgrn_decay(g_ref, b_ref, reverse):
    g = g_ref[0]
    hi = g.astype(BF16)
    lo = (g - hi.astype(F32)).astype(BF16)
    u01 = _one_zero(_seen(reverse))
    b_ref[...] = _dot(u01, hi) + _dot(u01, lo)


def _chunk_end(b_ref, reverse):
    return b_ref[0:1, :] if reverse else b_ref[HGRN_CHUNK - 1:HGRN_CHUNK, :]


def _decay_columns(b_end):
    e_end = jnp.exp(b_end)
    rows = [e_end[:, h * HEAD_DIM:(h + 1) * HEAD_DIM] for h in range(N_HEADS)]
    return jnp.concatenate(rows + [jnp.zeros((HEAD_DIM - N_HEADS, HEAD_DIM), F32)], axis=0).T


def _hgrn_fast(q_ref, v_ref, k_ref, b_ref, o_ref, s_ref, reverse):
    t, bl = HGRN_CHUNK, HGRN_BLOCK
    nb = t // bl
    seen = _seen(reverse)
    edges = _block_edges(b_ref, reverse)
    gain = {(i, j): jnp.exp(edges[i] - edges[j])
            for i in range(nb) for j in range(nb) if i != j and _reads(i, j, reverse)}
    b_end = _chunk_end(b_ref, reverse)
    e_cols = _decay_columns(b_end)
    zblk = jnp.zeros((bl, HEAD_DIM), BF16)
    full = nb - 1 if reverse else 0
    heads = [slice(h * HEAD_DIM, (h + 1) * HEAD_DIM) for h in range(N_HEADS)]

    scores, q_full, k_state = [], [], []
    for hs in heads:
        binc = b_ref[:, hs]
        q = q_ref[0, :, hs].astype(F32)
        k = k_ref[0, :, hs].astype(F32)
        q_own, k_own = [], []
        for i in range(nb):
            bs = slice(i * bl, (i + 1) * bl)
            q_own.append(q[bs] * jnp.exp(binc[bs] - edges[i][:, hs]))
            k_own.append((k[bs] * jnp.exp(edges[i][:, hs] - binc[bs])).astype(BF16))
        q_slabs = []
        for j in range(nb):
            parts = []
            for i in range(nb):
                if not _reads(i, j, reverse):
                    parts.append(zblk)
                elif i == j:
                    parts.append(q_own[i].astype(BF16))
                else:
                    parts.append((q_own[i] * gain[i, j][:, hs]).astype(BF16))
            q_slabs.append(jnp.concatenate(parts, axis=0))
        k_slabs = [jnp.concatenate([k_own[i] if i == j else zblk for i in range(nb)], axis=0)
                   for j in range(nb)]
        near = slice(nb // 2, nb) if reverse else slice(0, nb // 2)
        near_rows = slice(t // 2, t) if reverse else slice(0, t // 2)
        far_rows = slice(0, t // 2) if reverse else slice(t // 2, t)
        s_near = _dot_nt(jnp.concatenate([s[near_rows] for s in q_slabs[near]], axis=1),
                         jnp.concatenate(k_slabs[near], axis=1))
        s_far = _dot_nt(jnp.concatenate([s[far_rows] for s in q_slabs], axis=1),
                        jnp.concatenate(k_slabs, axis=1))
        scores.append(jnp.concatenate([s_far, s_near] if reverse else [s_near, s_far], axis=0))
        q_full.append(q_slabs[full])
        k_state.append((k * jnp.exp(b_end[:, hs] - binc)).astype(BF16))

    for h, hs in enumerate(heads):
        p = jnp.where(seen, scores[h], 0.0).astype(BF16)
        lhs = jnp.concatenate([p, q_full[h]], axis=1)
        rhs = jnp.concatenate([v_ref[0, :, hs], s_ref[h].astype(BF16)], axis=0)
        o_ref[0, :, hs] = _dot(lhs, rhs).astype(o_ref.dtype)

    for h, hs in enumerate(heads):
        s_ref[h] = s_ref[h] * e_cols[:, h:h + 1] + _dot_tn(k_state[h], v_ref[0, :, hs])


def _hgrn_slow(q_ref, v_ref, k_ref, b_ref, o_ref, s_ref, oi_ref, kk_ref, vv_ref, reverse):
    t = HGRN_CHUNK
    q = q_ref[0].astype(F32)
    binc = b_ref[...]
    kk_ref[...] = k_ref[0].astype(F32)
    vv_ref[...] = v_ref[0].astype(F32)
    oi_ref[...] = jnp.zeros((t, HGRN_WIDTH), F32)
    trow = lax.broadcasted_iota(I32, (t, HGRN_WIDTH), 0)
    heads = [slice(h * HEAD_DIM, (h + 1) * HEAD_DIM) for h in range(N_HEADS)]

    def body(s, carry):
        reads = (trow <= s) if reverse else (trow >= s)
        w = jnp.exp(jnp.where(reads, binc - b_ref[pl.ds(s, 1), :], 0.0))
        prod = jnp.where(reads, q * kk_ref[pl.ds(s, 1), :] * w, 0.0)
        vs = vv_ref[pl.ds(s, 1), :]
        for hs in heads:
            oi_ref[:, hs] += jnp.sum(prod[:, hs], axis=1, keepdims=True) * vs[:, hs]
        return carry

    lax.fori_loop(0, t, body, 0)
    b_end = _chunk_end(b_ref, reverse)
    e_cols = _decay_columns(b_end)
    q_full = (q * jnp.exp(binc)).astype(BF16)
    k_state = (kk_ref[...] * jnp.exp(b_end - binc)).astype(BF16)
    for h, hs in enumerate(heads):
        o_ref[0, :, hs] = (oi_ref[:, hs] + _dot(q_full[:, hs], s_ref[h].astype(BF16))).astype(o_ref.dtype)
        s_ref[h] = s_ref[h] * e_cols[:, h:h + 1] + _dot_tn(k_state[:, hs], v_ref[0, :, hs])


def _hgrn_kernel(flag_ref, qf_ref, vf_ref, kf_ref, gf_ref, qb_ref, vb_ref, kb_ref, gb_ref, s0f_ref, s0b_ref,
                 of_ref, ob_ref, sf_ref, sb_ref, bf_ref, bb_ref, oi_ref, kk_ref, vv_ref):
    b = pl.program_id(0)
    j = pl.program_id(1)
    n = pl.num_programs(1)

    @pl.when(j == 0)
    def _():
        sf_ref[...] = s0f_ref[0]
        sb_ref[...] = s0b_ref[0]

    safe = (flag_ref[(b * n + j) * 2] + flag_ref[(b * n + n - 1 - j) * 2 + 1]) == 2

    @pl.when(safe)
    def _():
        _hgrn_decay(gf_ref, bf_ref, False)
        _hgrn_decay(gb_ref, bb_ref, True)
        _hgrn_fast(qf_ref, vf_ref, kf_ref, bf_ref, of_ref, sf_ref, False)
        _hgrn_fast(qb_ref, vb_ref, kb_ref, bb_ref, ob_ref, sb_ref, True)

    @pl.when(jnp.logical_not(safe))
    def _():
        _hgrn_decay(gf_ref, bf_ref, False)
        _hgrn_decay(gb_ref, bb_ref, True)
        _hgrn_slow(qf_ref, vf_ref, kf_ref, bf_ref, of_ref, sf_ref, oi_ref, kk_ref, vv_ref, False)
        _hgrn_slow(qb_ref, vb_ref, kb_ref, bb_ref, ob_ref, sb_ref, oi_ref, kk_ref, vv_ref, True)


def _hgrn(flags, q, v, kf, gf, kb, gb, s0f, s0b):
    b, l, w = q.shape
    t = HGRN_CHUNK
    n = l // t
    fwd = pl.BlockSpec((1, t, w), lambda i, j, f: (i, j, 0))
    bwd = pl.BlockSpec((1, t, w), lambda i, j, f: (i, n - 1 - j, 0))
    sspec = pl.BlockSpec((1, N_HEADS, HEAD_DIM, HEAD_DIM), lambda i, j, f: (i, 0, 0, 0))
    out = jax.ShapeDtypeStruct((b, l, w), BF16)
    state = pltpu.VMEM((N_HEADS, HEAD_DIM, HEAD_DIM), F32)
    rows = pltpu.VMEM((t, w), F32)
    grid_spec = pltpu.PrefetchScalarGridSpec(
        num_scalar_prefetch=1,
        grid=(b, n),
        in_specs=[fwd, fwd, fwd, fwd, bwd, bwd, bwd, bwd, sspec, sspec],
        out_specs=[fwd, bwd],
        scratch_shapes=[state, state, rows, rows, rows, rows, rows],
    )
    return pl.pallas_call(
        _hgrn_kernel,
        grid_spec=grid_spec,
        out_shape=[out, out],
        compiler_params=_params(("arbitrary", "arbitrary")),
        name="hgrn",
    )(flags, q, v, kf, gf, q, v, kb, gb, s0f, s0b)


def _pool_col_matrices():
    mats = np.zeros((len(POOL_WINDOWS), POOL_TILE, POOL_TILE), np.float32)
    for j, w in enumerate(POOL_WINDOWS):
        for t in range(POOL_TILE):
            r, c = divmod(t, GRID_W)
            lo, hi = max(c - w // 2, 0), min(c + w // 2 - 1, GRID_W - 1)
            mats[j, t, r * GRID_W + lo:r * GRID_W + hi + 1] = 1.0
    return jnp.asarray(mats, BF16)


def _window_len(pos, half, n):
    return jnp.minimum(pos + half - 1, n - 1) + 1 - jnp.maximum(pos - half, 0)


def _pool_kernel(p_ref, a_ref, wp_ref, ps_ref, o_ref, pad_ref, inr_ref, inc_ref, *, rows):
    j = pl.program_id(1)
    half = jnp.left_shift(1, j)
    gw = GRID_W
    halo = POOL_HALO * gw
    l = rows * gw
    tile_rows = POOL_TILE // gw
    edge = POOL_HALO // 2
    for buf in range(2):
        pad_ref[buf, 0:halo, :] = jnp.zeros((halo, POOL_GROUP), F32)
        pad_ref[buf, halo + l:2 * halo + l, :] = jnp.zeros((halo, POOL_GROUP), F32)
    pad_ref[0, halo:halo + l, :] = p_ref[0]

    def level(src, dst, back, fwd):
        def body(g, carry):
            for u in range(8):
                t = pl.multiple_of((edge + g * 8 + u) * gw, gw)
                pad_ref[dst, pl.ds(t, gw), :] = (pad_ref[src, pl.ds(t - back * gw, gw), :]
                                                 + pad_ref[src, pl.ds(t + fwd * gw, gw), :])
            return carry

        lax.fori_loop(0, (rows + 2 * (POOL_HALO - edge)) // 8, body, 0)

    level(0, 1, 1, 0)

    @pl.when(j >= 1)
    def _():
        level(1, 0, 1, 1)

    @pl.when(j >= 2)
    def _():
        level(0, 1, 2, 2)

    @pl.when(j >= 3)
    def _():
        level(1, 0, 4, 4)

    summed = 1 - jnp.bitwise_and(j, 1)
    r = lax.broadcasted_iota(I32, inr_ref.shape, 0)
    inr_ref[...] = 1.0 / _window_len(r, half, rows).astype(F32)
    c = jnp.bitwise_and(lax.broadcasted_iota(I32, inc_ref.shape, 0), gw - 1)
    inc_ref[...] = 1.0 / _window_len(c, half, gw).astype(F32)
    a = a_ref[0]
    wp = wp_ref[0]
    scale = ps_ref[0]

    def step(i, carry):
        tiles = [POOL_UNROLL * i + u for u in range(POOL_UNROLL)]
        starts = [pl.multiple_of(t * POOL_TILE, POOL_TILE) for t in tiles]
        sums = []
        for t0 in starts:
            rsum = pad_ref[summed, pl.ds(halo + t0, POOL_TILE), :]
            hi = rsum.astype(BF16)
            lo = (rsum - hi.astype(F32)).astype(BF16)
            sums.append(_dot(a, jnp.concatenate([hi, lo], axis=1)))
        resid = []
        for t, t0, both in zip(tiles, starts, sums):
            total = both[:, :POOL_GROUP] + both[:, POOL_GROUP:]
            by_row = [total[rr * gw:(rr + 1) * gw] * inr_ref[pl.ds(t * tile_rows + rr, 1), :]
                      for rr in range(tile_rows)]
            resid.append((jnp.concatenate(by_row, axis=0) * inc_ref[...]
                          - p_ref[0, pl.ds(t0, POOL_TILE), :]).astype(BF16))
        for t0, rb in zip(starts, resid):
            o_ref[0, pl.ds(t0, POOL_TILE), :] = (_dot(rb, wp) * scale).astype(BF16)
        return carry

    lax.fori_loop(0, l // (POOL_UNROLL * POOL_TILE), step, 0)


def _pool(p, w_pool_bf, pool_scale):
    b, l, _ = p.shape
    rows = l // GRID_W
    ng = len(POOL_WINDOWS)
    grp = pl.BlockSpec((1, l, POOL_GROUP), lambda i, j: (i, 0, j))
    return pl.pallas_call(
        functools.partial(_pool_kernel, rows=rows),
        grid=(b, ng),
        in_specs=[grp,
                  pl.BlockSpec((1, POOL_TILE, POOL_TILE), lambda i, j: (j, 0, 0)),
                  pl.BlockSpec((1, POOL_GROUP, POOL_GROUP), lambda i, j: (j, 0, 0)),
                  pl.BlockSpec((1, 1, POOL_GROUP), lambda i, j: (j, 0, 0))],
        out_specs=grp,
        out_shape=jax.ShapeDtypeStruct((b, l, POOL_WIDTH), BF16),
        scratch_shapes=[pltpu.VMEM((2, (rows + 2 * POOL_HALO) * GRID_W, POOL_GROUP), F32),
                        pltpu.VMEM((rows, POOL_GROUP), F32),
                        pltpu.VMEM((POOL_TILE, POOL_GROUP), F32)],
        compiler_params=_params(("arbitrary", "arbitrary")),
        name="pool",
    )(p, _pool_col_matrices(), w_pool_bf, pool_scale.reshape(ng, 1, POOL_GROUP))


def _merge_kernel(x_ref, of_ref, ob_ref, og_ref, pl_ref, ga_ref, gb_ref, mod_ref, hg_ref, n2_ref,
                  wa_ref, wb_ref, wo_ref, wr_ref, nh_ref, vx_ref, pr_ref):
    b = pl.program_id(0)
    m = mod_ref[pl.ds(b, 1), :]
    g1 = m[:, 2 * D_MODEL:3 * D_MODEL]
    sh2 = m[:, 3 * D_MODEL:4 * D_MODEL]
    sc2 = m[:, 4 * D_MODEL:5 * D_MODEL]
    hg = hg_ref[...]
    for r0 in range(0, x_ref.shape[1], MERGE_ROWS):
        rs = slice(r0, r0 + MERGE_ROWS)
        gated = []
        for h in range(N_HEADS):
            hs = slice(h * HEAD_DIM, (h + 1) * HEAD_DIM)
            o = of_ref[0, rs, hs].astype(F32) + ob_ref[0, rs, hs].astype(F32)
            gated.append((_rms(o, hg) * og_ref[0, rs, hs].astype(F32)).astype(BF16))
        ya = _dot(jnp.concatenate(gated, axis=1), wa_ref[...])
        yb = _dot(pl_ref[0, rs, :], wb_ref[...])
        y = ga_ref[0, rs, :].astype(F32) * ya + gb_ref[0, rs, :].astype(F32) * yb
        nh = x_ref[0, rs, :] + g1 * _dot(y.astype(BF16), wo_ref[...])
        nh_ref[0, rs, :] = nh
        vx = (_rms(nh, n2_ref[...]) * (1.0 + sc2) + sh2).astype(BF16)
        vx_ref[0, rs, :] = vx
        logits = _dot_nt(wr_ref[...], vx)
        ex = jnp.exp(logits - jnp.max(logits, axis=0, keepdims=True))
        pr_ref[0, :, rs] = ex / jnp.sum(ex, axis=0, keepdims=True)


def _merge(x, o_f, o_b, og, pooled, ga, gb, mod, hgrn_g, norm2_g, w_a_bf, w_b_bf, w_out_bf, w_router_t):
    b, l, d = x.shape
    tm = PROJ_TILE
    tok = lambda w: pl.BlockSpec((1, tm, w), lambda i, j: (i, j, 0))
    const = lambda a: pl.BlockSpec(a.shape, lambda i, j: (0,) * a.ndim)
    return pl.pallas_call(
        _merge_kernel,
        grid=(b, l // tm),
        in_specs=[tok(d), tok(d), tok(d), tok(d), tok(POOL_WIDTH), tok(d), tok(d),
                  const(mod), const(hgrn_g), const(norm2_g),
                  _resident(w_a_bf.shape, lambda i, j: (0, 0)),
                  _resident(w_b_bf.shape, lambda i, j: (0, 0)),
                  _resident(w_out_bf.shape, lambda i, j: (0, 0)),
                  const(w_router_t)],
        out_specs=[tok(d), tok(d), pl.BlockSpec((1, N_EXPERTS, tm), lambda i, j: (i, 0, j))],
        out_shape=[jax.ShapeDtypeStruct((b, l, d), F32),
                   jax.ShapeDtypeStruct((b, l, d), BF16),
                   jax.ShapeDtypeStruct((b, N_EXPERTS, l), F32)],
        compiler_params=_params(("arbitrary", "arbitrary")),
        name="merge",
    )(x, o_f, o_b, og, pooled, ga, gb, mod, hgrn_g, norm2_g, w_a_bf, w_b_bf, w_out_bf, w_router_t)


def _route_kernel(p_ref, code_ref, start_ref, *, cap):
    l = p_ref.shape[2]
    tm = TOKEN_TILE
    bits = lax.bitcast_convert_type(p_ref[0], I32)
    capf = jnp.float32(cap)

    def count_ge(cand):
        return jnp.sum(jnp.where(bits >= cand, 1.0, 0.0), axis=1, keepdims=True)

    def search(i, prefix):
        cand = jnp.bitwise_or(prefix, jnp.left_shift(1, 30 - i))
        return jnp.where(count_ge(cand) >= capf, cand, prefix)

    tau = lax.fori_loop(0, 31, search, jnp.zeros((N_EXPERTS, 1), I32))
    need = capf - jnp.sum(jnp.where(bits > tau, 1.0, 0.0), axis=1, keepdims=True)
    before = _one_zero(lax.broadcasted_iota(I32, (tm, tm), 0) < lax.broadcasted_iota(I32, (tm, tm), 1))
    lane = lax.broadcasted_iota(I32, (N_EXPERTS, 128), 1)

    def tile(k, carry):
        n_gt, n_eq, starts = carry
        sl = pl.ds(pl.multiple_of(k * tm, tm), tm)
        bk = lax.bitcast_convert_type(p_ref[0, :, sl], I32)
        gt = bk > tau
        eq = bk == tau
        gt01 = _one_zero(gt)
        eq01 = _one_zero(eq)
        eq_before = n_eq + _dot(eq01, before)
        chosen = jnp.logical_or(gt, jnp.logical_and(eq, eq_before < need))
        rank = n_gt + _dot(gt01, before) + jnp.minimum(eq_before, need)
        code_ref[0, :, sl] = jnp.where(chosen, rank, -1.0).astype(I32)
        starts = jnp.where(lane == k, (n_gt + jnp.minimum(n_eq, need)).astype(I32), starts)
        n_gt = n_gt + jnp.sum(gt01.astype(F32), axis=1, keepdims=True)
        n_eq = n_eq + jnp.sum(eq01.astype(F32), axis=1, keepdims=True)
        return n_gt, n_eq, starts

    zero = jnp.zeros((N_EXPERTS, 1), F32)
    _, _, starts = lax.fori_loop(0, l // tm, tile, (zero, zero, jnp.zeros((N_EXPERTS, 128), I32)))
    start_ref[0] = jnp.where(lane == l // tm, cap, starts)


def _route(probs_t, cap):
    b, e, l = probs_t.shape
    assert l // TOKEN_TILE < 128
    return pl.pallas_call(
        functools.partial(_route_kernel, cap=cap),
        grid=(b,),
        in_specs=[pl.BlockSpec((1, e, l), lambda i: (i, 0, 0))],
        out_specs=[pl.BlockSpec((1, e, l), lambda i: (i, 0, 0)),
                   pl.BlockSpec((1, e, 128), lambda i: (i, 0, 0))],
        out_shape=[jax.ShapeDtypeStruct((b, e, l), I32), jax.ShapeDtypeStruct((b, e, 128), I32)],
        compiler_params=_params(("arbitrary",)),
        name="route",
    )(probs_t)


def _floor8(n):
    return jnp.left_shift(jnp.right_shift(n, 3), 3)


def _dispatch_kernel(start_ref, vx_ref, code_ref, xg_hbm, stage_ref, carry_ref, sent_ref, sem,
                     *, cap, n_tiles, n_steps):
    b = pl.program_id(0)
    k = pl.program_id(1)
    step = b * n_tiles + k
    w = SLOT_WINDOW
    tm = TOKEN_TILE
    ne = N_EXPERTS

    @pl.when(step == 0)
    def _():
        sent_ref[0] = 0

    @pl.when(k == 0)
    def _():
        carry_ref[...] = jnp.zeros(carry_ref.shape, F32)

    base, tail = [], []
    n_pass = jnp.int32(1)
    for e in range(ne):
        i0 = (b * ne + e) * 128 + k
        end = start_ref[i0 + 1]
        base.append(_floor8(start_ref[i0]))
        tail.append(_floor8(end))
        n_pass = jnp.maximum(n_pass, jnp.right_shift(end - base[e] + w - 1, SLOT_SHIFT))

    def send(buf, rows):
        @pl.when(sent_ref[0] > 0)
        def _():
            for e in range(ne):
                pltpu.make_async_copy(stage_ref.at[0, pl.ds(e * w, w), :], xg_hbm.at[b, e, pl.ds(0, w), :],
                                      sem.at[e]).wait()

        for e in range(ne):
            pltpu.make_async_copy(stage_ref.at[buf, pl.ds(e * w, w), :],
                                  xg_hbm.at[b, e, pl.ds(pl.multiple_of(rows[e], 8), w), :], sem.at[e]).start()
        sent_ref[0] = sent_ref[0] + 1

    slot = lax.broadcasted_iota(I32, (w, tm), 0)
    group = lax.broadcasted_iota(I32, (8, tm), 0)

    def one_pass(c, carry):
        buf = jnp.bitwise_and(sent_ref[0], 1)
        hits = [_one_zero(code_ref[0, e:e + 1, :] == (slot + (base[e] + c * w))) for e in range(ne)]
        tails = [_one_zero(code_ref[0, e:e + 1, :] == (group + tail[e])) for e in range(ne)]
        rows = _dot(jnp.concatenate(hits + tails, axis=0), vx_ref[0])
        stage_ref[buf] = rows[:ne * w].astype(BF16)

        @pl.when(c == 0)
        def _():
            for e in range(ne):
                first = rows[e * w:e * w + 8] + carry_ref[e]
                stage_ref[buf, e * w:e * w + 8, :] = first.astype(BF16)
                own = rows[ne * w + 8 * e:ne * w + 8 * e + 8]
                carry_ref[e] = jnp.where(tail[e] == base[e], carry_ref[e], 0.0) + own

        send(buf, [jnp.minimum(base[e] + c * w, cap) for e in range(ne)])
        return carry

    lax.fori_loop(0, n_pass, one_pass, 0)

    @pl.when(k == n_tiles - 1)
    def _():
        buf = jnp.bitwise_and(sent_ref[0], 1)
        stage_ref[buf] = jnp.zeros(stage_ref.shape[1:], BF16)
        send(buf, [cap] * ne)

    @pl.when(step == n_steps - 1)
    def _():
        for e in range(ne):
            pltpu.make_async_copy(stage_ref.at[0, pl.ds(e * w, w), :], xg_hbm.at[b, e, pl.ds(0, w), :],
                                  sem.at[e]).wait()


def _dispatch(starts_flat, vx, code, cap):
    b, l, d = vx.shape
    tm = TOKEN_TILE
    n_tiles = l // tm
    grid_spec = pltpu.PrefetchScalarGridSpec(
        num_scalar_prefetch=1,
        grid=(b, n_tiles),
        in_specs=[pl.BlockSpec((1, tm, d), lambda i, j, s: (i, j, 0)),
                  pl.BlockSpec((1, N_EXPERTS, tm), lambda i, j, s: (i, 0, j))],
        out_specs=pl.BlockSpec(memory_space=pl.ANY),
        scratch_shapes=[pltpu.VMEM((2, N_EXPERTS * SLOT_WINDOW, d), BF16),
                        pltpu.VMEM((N_EXPERTS, 8, d), F32),
                        pltpu.SMEM((1,), I32),
                        pltpu.SemaphoreType.DMA((N_EXPERTS,))],
    )
    return pl.pallas_call(
        functools.partial(_dispatch_kernel, cap=cap, n_tiles=n_tiles, n_steps=b * n_tiles),
        grid_spec=grid_spec,
        out_shape=jax.ShapeDtypeStruct((b, N_EXPERTS, cap + SLOT_WINDOW, d), BF16),
        compiler_params=_params(("arbitrary", "arbitrary")),
        name="dispatch",
    )(starts_flat, vx, code)


def _moe_kernel(xg_ref, wg_ref, wu_ref, wd_ref, y_ref, wgb_ref, wub_ref, wdb_ref, *, cap):
    @pl.when(pl.program_id(1) == 0)
    def _():
        wgb_ref[...] = wg_ref[0].astype(BF16)
        wub_ref[...] = wu_ref[0].astype(BF16)
        wdb_ref[...] = wd_ref[0].astype(BF16)

    fr = min(FFN_ROWS, cap)
    for r0 in range(0, cap, fr):
        xg = xg_ref[0, 0, r0:r0 + fr, :]
        hid = (_silu(_dot(xg, wgb_ref[...])) * _dot(xg, wub_ref[...])).astype(BF16)
        y_ref[0, 0, r0:r0 + fr, :] = _dot(hid, wdb_ref[...]).astype(BF16)


def _moe(xg, w_gate, w_up, w_down, cap):
    b, ne, _, d = xg.shape
    ff = w_gate.shape[2]
    rows = pl.BlockSpec((1, 1, cap, d), lambda e, i: (i, e, 0, 0))
    return pl.pallas_call(
        functools.partial(_moe_kernel, cap=cap),
        grid=(ne, b),
        in_specs=[rows,
                  pl.BlockSpec((1, d, ff), lambda e, i: (e, 0, 0)),
                  pl.BlockSpec((1, d, ff), lambda e, i: (e, 0, 0)),
                  pl.BlockSpec((1, ff, d), lambda e, i: (e, 0, 0))],
        out_specs=rows,
        out_shape=jax.ShapeDtypeStruct((b, ne, cap, d), BF16),
        scratch_shapes=[pltpu.VMEM((d, ff), BF16), pltpu.VMEM((d, ff), BF16), pltpu.VMEM((ff, d), BF16)],
        compiler_params=_params(("arbitrary", "arbitrary")),
        name="moe",
    )(xg, w_gate, w_up, w_down)


def _combine_kernel(start_ref, nh_ref, code_ref, p_ref, mod_ref, fg_ref, y_hbm, o_ref, ybuf, sem,
                    *, cap, n_tiles, n_steps):
    b = pl.program_id(0)
    k = pl.program_id(1)
    step = b * n_tiles + k
    tm = TOKEN_TILE
    w = SLOT_WINDOW

    def window(bb, kk, e, c):
        st = start_ref[(bb * N_EXPERTS + e) * 128 + kk]
        lo = jnp.left_shift(jnp.right_shift(st, 3), 3) + c * w
        return lo, pl.multiple_of(jnp.minimum(lo, cap - w), 8)

    def copies(bb, kk, buf):
        out = []
        for e in range(N_EXPERTS):
            _, src = window(bb, kk, e, 0)
            out.append(pltpu.make_async_copy(y_hbm.at[bb, e, pl.ds(src, w), :], ybuf.at[buf, e], sem.at[buf, e]))
        return out

    @pl.when(step == 0)
    def _():
        for cp in copies(b, k, 0):
            cp.start()

    cur = lax.rem(step, 2)

    @pl.when(step + 1 < n_steps)
    def _():
        nk = jnp.where(k + 1 == n_tiles, 0, k + 1)
        nb = jnp.where(k + 1 == n_tiles, b + 1, b)
        for cp in copies(nb, nk, 1 - cur):
            cp.start()

    for cp in copies(b, k, cur):
        cp.wait()

    slot = lax.broadcasted_iota(I32, (w, tm), 0)
    p = p_ref[0]

    def scatter(c, buf_rows):
        weights = []
        for e in range(N_EXPERTS):
            lo, src = window(b, k, e, c)
            codes = code_ref[0, e:e + 1, :]
            hit = jnp.logical_and(codes == (slot + src), jnp.logical_and(codes >= lo, codes < lo + w))
            weights.append(jnp.where(hit, p[e:e + 1, :], 0.0).astype(BF16))
        return _dot_tn(jnp.concatenate(weights, axis=0), buf_rows)

    acc = scatter(0, ybuf[cur].reshape(N_EXPERTS * w, D_MODEL))

    most = jnp.int32(0)
    for e in range(N_EXPERTS):
        i0 = (b * N_EXPERTS + e) * 128 + k
        st = start_ref[i0]
        span = st - jnp.left_shift(jnp.right_shift(st, 3), 3) + start_ref[i0 + 1] - st
        most = jnp.maximum(most, jnp.right_shift(span + w - 1, SLOT_SHIFT))

    def extra(c, acc):
        cps = []
        for e in range(N_EXPERTS):
            _, src = window(b, k, e, c)
            cps.append(pltpu.make_async_copy(y_hbm.at[b, e, pl.ds(src, w), :], ybuf.at[cur, e], sem.at[cur, e]))
        for cp in cps:
            cp.start()
        for cp in cps:
            cp.wait()
        return acc + scatter(c, ybuf[cur].reshape(N_EXPERTS * w, D_MODEL))

    acc = lax.fori_loop(1, most, extra, acc)
    g2 = mod_ref[pl.ds(b, 1), 5 * D_MODEL:6 * D_MODEL]
    o_ref[0] = _rms(nh_ref[0] + g2 * acc, fg_ref[...])


def _combine(starts_flat, new_hx, code, probs_t, mod, final_g, y, cap):
    b, l, d = new_hx.shape
    tm = TOKEN_TILE
    n_tiles = l // tm
    grid_spec = pltpu.PrefetchScalarGridSpec(
        num_scalar_prefetch=1,
        grid=(b, n_tiles),
        in_specs=[pl.BlockSpec((1, tm, d), lambda i, j, s: (i, j, 0)),
                  pl.BlockSpec((1, N_EXPERTS, tm), lambda i, j, s: (i, 0, j)),
                  pl.BlockSpec((1, N_EXPERTS, tm), lambda i, j, s: (i, 0, j)),
                  pl.BlockSpec(mod.shape, lambda i, j, s: (0, 0)),
                  pl.BlockSpec((1, d), lambda i, j, s: (0, 0)),
                  pl.BlockSpec(memory_space=pl.ANY)],
        out_specs=pl.BlockSpec((1, tm, d), lambda i, j, s: (i, j, 0)),
        scratch_shapes=[pltpu.VMEM((2, N_EXPERTS, SLOT_WINDOW, d), BF16),
                        pltpu.SemaphoreType.DMA((2, N_EXPERTS))],
    )
    return pl.pallas_call(
        functools.partial(_combine_kernel, cap=cap, n_tiles=n_tiles, n_steps=b * n_tiles),
        grid_spec=grid_spec,
        out_shape=jax.ShapeDtypeStruct((b, l, d), F32),
        compiler_params=_params(("arbitrary", "arbitrary")),
        name="combine",
    )(starts_flat, new_hx, code, probs_t, mod, final_g, y)


def _layer(x, c, ctx, c_ctx, lb_logits, w_mod, b_mod, norm1_g, w_in, hgrn_norm_g, w_a, w_pool, pool_scale,
           w_b, w_out, norm2_g, w_router, w_e_gate, w_e_up, w_e_down, final_g):
    b, l, d = x.shape
    cap = CAPACITY_FACTOR * l // N_EXPERTS
    assert b < MOD_ROWS and l % HGRN_CHUNK == 0 and l % PROJ_TILE == 0 and l % TOKEN_TILE == 0
    assert cap % 8 == 0 and cap >= SLOT_WINDOW and l % (POOL_UNROLL * POOL_TILE) == 0
    cc = jnp.zeros((MOD_ROWS, d), F32).at[:b].set(c).at[b].set(c_ctx)
    assert lb_logits.shape == (2, 2, HGRN_WIDTH)
    lbl = lb_logits.reshape(4, HGRN_WIDTH)
    row = lambda a: a.reshape(1, -1)
    w_in_bf = w_in.astype(BF16)

    mod = _adaln(cc, w_mod, b_mod)
    s0f, s0b = _ctx_states(ctx, row(norm1_g), mod, lbl, w_in_bf)
    q, v, kf, gf, kb, gb, og, p, ga, gbm, safe = _in_proj(x, row(norm1_g), mod, lbl, w_in_bf)
    per_tile = TOKEN_TILE // HGRN_CHUNK
    safe = safe[:, :, :2 * per_tile, 0].reshape(b, -1, 2, per_tile)
    flags = safe.transpose(0, 1, 3, 2).astype(I32).reshape(-1)
    o_f, o_b = _hgrn(flags, q, v, kf, gf, kb, gb, s0f, s0b)
    pooled = _pool(p, w_pool.astype(BF16), pool_scale)
    new_hx, vx, probs_t = _merge(x, o_f, o_b, og, pooled, ga, gbm, mod, row(hgrn_norm_g), row(norm2_g),
                                 w_a.astype(BF16), w_b.astype(BF16), w_out.astype(BF16),
                                 w_router.T.astype(BF16))
    code, starts = _route(probs_t, cap)
    starts_flat = starts.reshape(-1)
    xg = _dispatch(starts_flat, vx, code, cap)
    y = _moe(xg, w_e_gate, w_e_up, w_e_down, cap)
    return _combine(starts_flat, new_hx, code, probs_t, mod, row(final_g), y, cap)


def kernel(x, c, ctx, c_ctx, lb_logits, w_mod, b_mod, norm1_g, w_in, hgrn_norm_g, w_a, w_pool, pool_scale,
           w_b, w_out, norm2_g, w_router, w_e_gate, w_e_up, w_e_down, final_g):
    assert w_mod.shape[0] == 1, "single-layer trunk: the context stream only seeds the latent recurrence"
    return _layer(x, c, ctx, c_ctx, lb_logits, w_mod[0], b_mod[0], norm1_g[0], w_in[0], hgrn_norm_g[0], w_a[0],
                  w_pool[0], pool_scale[0], w_b[0], w_out[0], norm2_g[0], w_router[0], w_e_gate[0],
                  w_e_up[0], w_e_down[0], final_g)
```

```python
import functools

import numpy as np
import jax
import jax.numpy as jnp
from jax import lax
from jax.experimental import pallas as pl
from jax.experimental.pallas import tpu as pltpu

F32 = jnp.float32
BF16 = jnp.bfloat16
I32 = jnp.int32

D_MODEL = 1024
N_HEADS = 8
HEAD_DIM = 128
HGRN_WIDTH = N_HEADS * HEAD_DIM
GRID_W = 64
POOL_WINDOWS = (2, 4, 8, 16)
POOL_GROUP = 128
POOL_WIDTH = POOL_GROUP * len(POOL_WINDOWS)
N_EXPERTS = 16
CAPACITY_FACTOR = 2
EXPERT_FF = 1024
EPS = 1e-6

Q_OFF = 0
I_OFF = Q_OFF + HGRN_WIDTH
FF_OFF = I_OFF + HGRN_WIDTH
FB_OFF = FF_OFF + HGRN_WIDTH
OG_OFF = FB_OFF + HGRN_WIDTH
P_OFF = OG_OFF + HGRN_WIDTH
GA_OFF = P_OFF + POOL_WIDTH
GB_OFF = GA_OFF + D_MODEL
IN_COLS = GB_OFF + D_MODEL

MOD_ROWS = 16
PROJ_TILE = 512
MERGE_ROWS = 512
TOKEN_TILE = 256
HGRN_CHUNK = 128
HGRN_BLOCK = 32
HGRN_STEP_CHUNKS = 2
HGRN_SAFE_DECAY = 80.0
POOL_HALO = 16
POOL_TILE = 256
POOL_UNROLL = 4
SLOT_SHIFT = 6
SLOT_WINDOW = 1 << SLOT_SHIFT
FFN_ROWS = 256
COMBINE_AHEAD = 2
VMEM_LIMIT = 56 * 1024 * 1024


def _dot(a, b):
    return jnp.dot(a, b, preferred_element_type=F32)


def _dot_nt(a, b):
    return lax.dot_general(a, b, (((1,), (1,)), ((), ())), preferred_element_type=F32)


def _dot_tn(a, b):
    return lax.dot_general(a, b, (((0,), (0,)), ((), ())), preferred_element_type=F32)


def _rms(x, g):
    ms = jnp.mean(x * x, axis=-1, keepdims=True)
    return x * lax.rsqrt(ms + EPS) * g


def _silu(z):
    return z * jax.nn.sigmoid(z)


def _lower_bound(a0, a1):
    m = jnp.maximum(a0, a1)
    e0 = jnp.exp(a0 - m)
    e1 = jnp.exp(a1 - m)
    return e0 / (e0 + e1)


def _split3(g):
    hi = g.astype(BF16)
    r = g - hi.astype(F32)
    mid = r.astype(BF16)
    lo = (r - mid.astype(F32)).astype(BF16)
    return hi, mid, lo


def _apply01(u01, g):
    hi, mid, lo = _split3(g)
    return _dot(u01, hi) + _dot(u01, mid) + _dot(u01, lo)


def _one_zero(mask):
    return jnp.where(mask, 1.0, 0.0).astype(BF16)


def _params(sem):
    return pltpu.CompilerParams(dimension_semantics=sem, vmem_limit_bytes=VMEM_LIMIT)


def _resident(shape, index_map):
    return pl.BlockSpec(shape, index_map, pipeline_mode=pl.Buffered(1))


def _adaln_kernel(c_ref, w_ref, b_ref, o_ref):
    c = c_ref[...]
    o_ref[...] = _dot(_silu(c).astype(BF16), w_ref[...].astype(BF16)) + b_ref[...]


def _adaln(cc, w_mod, b_mod):
    n = w_mod.shape[1]
    tn = 768
    return pl.pallas_call(
        _adaln_kernel,
        grid=(n // tn,),
        in_specs=[pl.BlockSpec((MOD_ROWS, D_MODEL), lambda j: (0, 0)),
                  pl.BlockSpec((D_MODEL, tn), lambda j: (0, j)),
                  pl.BlockSpec((1, tn), lambda j: (0, j))],
        out_specs=pl.BlockSpec((MOD_ROWS, tn), lambda j: (0, j)),
        out_shape=jax.ShapeDtypeStruct((MOD_ROWS, n), F32),
        compiler_params=_params(("arbitrary",)),
        name="adaln",
    )(cc, w_mod, b_mod.reshape(1, n))


def _ctx_kernel(ctx_ref, g1_ref, mod_ref, lbl_ref, wi_ref, wf_ref, wb_ref, sf_ref, sb_ref, *, ctx_row):
    x = ctx_ref[0]
    n = x.shape[0]
    sh = mod_ref[ctx_row:ctx_row + 1, 0:D_MODEL]
    sc = mod_ref[ctx_row:ctx_row + 1, D_MODEL:2 * D_MODEL]
    u = (_rms(x, g1_ref[...]) * (1.0 + sc) + sh).astype(BF16)
    v = _dot(u, wi_ref[...]).astype(BF16)
    zf = _dot(u, wf_ref[...])
    zb = _dot(u, wb_ref[...])
    lbf = _lower_bound(lbl_ref[0:1, :], lbl_ref[1:2, :])
    lbb = _lower_bound(lbl_ref[2:3, :], lbl_ref[3:4, :])
    ff = lbf + (1.0 - lbf) * jax.nn.sigmoid(zf)
    fb = lbb + (1.0 - lbb) * jax.nn.sigmoid(zb)
    r = lax.broadcasted_iota(I32, (n, n), 0)
    c = lax.broadcasted_iota(I32, (n, n), 1)
    ef = _apply01(_one_zero(c > r), jnp.log(ff))
    eb = _apply01(_one_zero(c < r), jnp.log(fb))
    kf = ((1.0 - ff) * jnp.exp(ef)).astype(BF16)
    kb = ((1.0 - fb) * jnp.exp(eb)).astype(BF16)
    for h in range(N_HEADS):
        hs = slice(h * HEAD_DIM, (h + 1) * HEAD_DIM)
        sf_ref[0, h] = _dot_tn(kf[:, hs], v[:, hs])
        sb_ref[0, h] = _dot_tn(kb[:, hs], v[:, hs])


def _ctx_states(ctx, norm1_g, mod, lbl, w_in_bf):
    b, lc, d = ctx.shape
    col = lambda k: pl.BlockSpec((d, HGRN_WIDTH), lambda i, k=k: (0, k))
    state = jax.ShapeDtypeStruct((b, N_HEADS, HEAD_DIM, HEAD_DIM), F32)
    sspec = pl.BlockSpec((1, N_HEADS, HEAD_DIM, HEAD_DIM), lambda i: (i, 0, 0, 0))
    return pl.pallas_call(
        functools.partial(_ctx_kernel, ctx_row=b),
        grid=(b,),
        in_specs=[pl.BlockSpec((1, lc, d), lambda i: (i, 0, 0)),
                  pl.BlockSpec((1, d), lambda i: (0, 0)),
                  pl.BlockSpec(mod.shape, lambda i: (0, 0)),
                  pl.BlockSpec(lbl.shape, lambda i: (0, 0)),
                  col(I_OFF // HGRN_WIDTH), col(FF_OFF // HGRN_WIDTH), col(FB_OFF // HGRN_WIDTH)],
        out_specs=[sspec, sspec],
        out_shape=[state, state],
        compiler_params=_params(("arbitrary",)),
        name="ctx_state",
    )(ctx, norm1_g, mod, lbl, w_in_bf, w_in_bf, w_in_bf)


def _chunks_safe(logf):
    out = []
    for c0 in range(0, logf.shape[0], HGRN_CHUNK):
        worst = None
        for lo in range(c0, c0 + HGRN_CHUNK, HGRN_BLOCK):
            dec = jnp.sum(logf[lo:lo + HGRN_BLOCK], axis=0, keepdims=True)
            worst = dec if worst is None else jnp.minimum(worst, dec)
        ok = jnp.min(worst, axis=1, keepdims=True) >= -HGRN_SAFE_DECAY
        out.append(jnp.broadcast_to(jnp.where(ok, 1.0, 0.0), (1, 128)))
    return out


def _inproj_kernel(x_ref, g1_ref, mod_ref, lbl_ref, w_ref,
                   q_ref, v_ref, kf_ref, gf_ref, kb_ref, gb_ref, og_ref, p_ref, ga_ref, gbm_ref, safe_ref):
    b = pl.program_id(0)
    x = x_ref[0]
    m = mod_ref[pl.ds(b, 1), :]
    sh = m[:, 0:D_MODEL]
    sc = m[:, D_MODEL:2 * D_MODEL]
    u = (_rms(x, g1_ref[...]) * (1.0 + sc) + sh).astype(BF16)

    z = _dot(u, w_ref[:, Q_OFF:I_OFF])
    q_ref[0] = _silu(z).astype(BF16)

    lbf = _lower_bound(lbl_ref[0:1, :], lbl_ref[1:2, :])
    f = lbf + (1.0 - lbf) * jax.nn.sigmoid(_dot(u, w_ref[:, FF_OFF:FB_OFF]))
    kf_ref[0] = (1.0 - f).astype(BF16)
    logf = jnp.log(f)
    gf_ref[0] = logf
    flags = _chunks_safe(logf)
    lbb = _lower_bound(lbl_ref[2:3, :], lbl_ref[3:4, :])
    f = lbb + (1.0 - lbb) * jax.nn.sigmoid(_dot(u, w_ref[:, FB_OFF:OG_OFF]))
    kb_ref[0] = (1.0 - f).astype(BF16)
    logf = jnp.log(f)
    gb_ref[0] = logf
    flags = flags + _chunks_safe(logf)
    safe_ref[0, 0] = jnp.concatenate(flags + [jnp.zeros((8 - len(flags), 128), F32)], axis=0)

    z = _dot(u, w_ref[:, OG_OFF:P_OFF])
    og_ref[0] = _silu(z).astype(BF16)
    ga_ref[0] = jax.nn.sigmoid(_dot(u, w_ref[:, GA_OFF:GB_OFF])).astype(BF16)
    gbm_ref[0] = jax.nn.sigmoid(_dot(u, w_ref[:, GB_OFF:IN_COLS])).astype(BF16)
    p_ref[0] = _dot(u, w_ref[:, P_OFF:GA_OFF])
    v_ref[0] = _dot(u, w_ref[:, I_OFF:FF_OFF]).astype(BF16)


def _in_proj(x, norm1_g, mod, lbl, w_in_bf):
    b, l, d = x.shape
    tm = TOKEN_TILE
    tok = lambda w: pl.BlockSpec((1, tm, w), lambda i, j: (i, j, 0))
    wide = lambda dt, w=HGRN_WIDTH: jax.ShapeDtypeStruct((b, l, w), dt)
    return pl.pallas_call(
        _inproj_kernel,
        grid=(b, l // tm),
        in_specs=[tok(d),
                  pl.BlockSpec((1, d), lambda i, j: (0, 0)),
                  pl.BlockSpec(mod.shape, lambda i, j: (0, 0)),
                  pl.BlockSpec(lbl.shape, lambda i, j: (0, 0)),
                  _resident((d, IN_COLS), lambda i, j: (0, 0))],
        out_specs=[tok(HGRN_WIDTH)] * 7 + [tok(POOL_WIDTH), tok(d), tok(d),
                                            pl.BlockSpec((1, 1, 8, 128), lambda i, j: (i, j, 0, 0))],
        out_shape=[wide(BF16), wide(BF16), wide(BF16), wide(F32), wide(BF16), wide(F32), wide(BF16),
                   wide(F32, POOL_WIDTH), wide(BF16, d), wide(BF16, d),
                   jax.ShapeDtypeStruct((b, l // tm, 8, 128), F32)],
        compiler_params=_params(("arbitrary", "arbitrary")),
        name="in_proj",
    )(x, norm1_g, mod, lbl, w_in_bf)


def _reads(i, j, reverse):
    return i <= j if reverse else i >= j


def _block_edges(b_ref, reverse):
    nb = HGRN_CHUNK // HGRN_BLOCK
    zero = jnp.zeros((1, HGRN_WIDTH), F32)
    out = []
    for j in range(nb):
        if reverse:
            r = (j + 1) * HGRN_BLOCK
            out.append(zero if j == nb - 1 else b_ref[r:r + 1, :])
        else:
            r = j * HGRN_BLOCK
            out.append(zero if j == 0 else b_ref[r - 1:r, :])
    return out


def _seen(reverse):
    t = HGRN_CHUNK
    row = lax.broadcasted_iota(I32, (t, t), 0)
    col = lax.broadcasted_iota(I32, (t, t), 1)
    return (col >= row) if reverse else (col <= row)


def _hgrn_decay(g_ref, b_ref, reverse):
    g = g_ref[...]
    hi = g.astype(BF16)
    lo = (g - hi.astype(F32)).astype(BF16)
    u01 = _one_zero(_seen(reverse))
    b_ref[...] = _dot(u01, hi) + _dot(u01, lo)


def _chunk_end(b_ref, reverse):
    return b_ref[0:1, :] if reverse else b_ref[HGRN_CHUNK - 1:HGRN_CHUNK, :]


def _decay_columns(b_end):
    e_end = jnp.exp(b_end)
    rows = [e_end[:, h * HEAD_DIM:(h + 1) * HEAD_DIM] for h in range(N_HEADS)]
    return jnp.concatenate(rows + [jnp.zeros((HEAD_DIM - N_HEADS, HEAD_DIM), F32)], axis=0).T


def _hgrn_fast(q_ref, v_ref, k_ref, b_ref, o_ref, s_ref, reverse):
    t, bl = HGRN_CHUNK, HGRN_BLOCK
    nb = t // bl
    seen = _seen(reverse)
    edges = _block_edges(b_ref, reverse)
    gain = {(i, j): jnp.exp(edges[i] - edges[j])
            for i in range(nb) for j in range(nb) if i != j and _reads(i, j, reverse)}
    b_end = _chunk_end(b_ref, reverse)
    e_cols = _decay_columns(b_end)
    zblk = jnp.zeros((bl, HEAD_DIM), BF16)
    full = nb - 1 if reverse else 0
    heads = [slice(h * HEAD_DIM, (h + 1) * HEAD_DIM) for h in range(N_HEADS)]

    scores, q_full, k_state = [], [], []
    for hs in heads:
        binc = b_ref[:, hs]
        q = q_ref[:, hs].astype(F32)
        k = k_ref[:, hs].astype(F32)
        q_own, k_own = [], []
        for i in range(nb):
            bs = slice(i * bl, (i + 1) * bl)
            q_own.append(q[bs] * jnp.exp(binc[bs] - edges[i][:, hs]))
            k_own.append((k[bs] * jnp.exp(edges[i][:, hs] - binc[bs])).astype(BF16))
        q_slabs = []
        for j in range(nb):
            parts = []
            for i in range(nb):
                if not _reads(i, j, reverse):
                    parts.append(zblk)
                elif i == j:
                    parts.append(q_own[i].astype(BF16))
                else:
                    parts.append((q_own[i] * gain[i, j][:, hs]).astype(BF16))
            q_slabs.append(jnp.concatenate(parts, axis=0))
        k_slabs = [jnp.concatenate([k_own[i] if i == j else zblk for i in range(nb)], axis=0)
                   for j in range(nb)]
        near = slice(nb // 2, nb) if reverse else slice(0, nb // 2)
        near_rows = slice(t // 2, t) if reverse else slice(0, t // 2)
        far_rows = slice(0, t // 2) if reverse else slice(t // 2, t)
        s_near = _dot_nt(jnp.concatenate([s[near_rows] for s in q_slabs[near]], axis=1),
                         jnp.concatenate(k_slabs[near], axis=1))
        s_far = _dot_nt(jnp.concatenate([s[far_rows] for s in q_slabs], axis=1),
                        jnp.concatenate(k_slabs, axis=1))
        scores.append(jnp.concatenate([s_far, s_near] if reverse else [s_near, s_far], axis=0))
        q_full.append(q_slabs[full])
        k_state.append((k * jnp.exp(b_end[:, hs] - binc)).astype(BF16))

    def readout():
        for h, hs in enumerate(heads):
            p = jnp.where(seen, scores[h], 0.0).astype(BF16)
            lhs = jnp.concatenate([p, q_full[h]], axis=1)
            rhs = jnp.concatenate([v_ref[:, hs], s_ref[h].astype(BF16)], axis=0)
            o_ref[:, hs] = _dot(lhs, rhs).astype(o_ref.dtype)

    def update():
        for h, hs in enumerate(heads):
            s_ref[h] = s_ref[h] * e_cols[:, h:h + 1] + _dot_tn(k_state[h], v_ref[:, hs])

    return readout, update


def _hgrn_slow(q_ref, v_ref, k_ref, b_ref, o_ref, s_ref, oi_ref, kk_ref, vv_ref, reverse):
    t = HGRN_CHUNK
    q = q_ref[...].astype(F32)
    binc = b_ref[...]
    kk_ref[...] = k_ref[...].astype(F32)
    vv_ref[...] = v_ref[...].astype(F32)
    oi_ref[...] = jnp.zeros((t, HGRN_WIDTH), F32)
    trow = lax.broadcasted_iota(I32, (t, HGRN_WIDTH), 0)
    heads = [slice(h * HEAD_DIM, (h + 1) * HEAD_DIM) for h in range(N_HEADS)]

    def body(s, carry):
        reads = (trow <= s) if reverse else (trow >= s)
        w = jnp.exp(jnp.where(reads, binc - b_ref[pl.ds(s, 1), :], 0.0))
        prod = jnp.where(reads, q * kk_ref[pl.ds(s, 1), :] * w, 0.0)
        vs = vv_ref[pl.ds(s, 1), :]
        for hs in heads:
            oi_ref[:, hs] += jnp.sum(prod[:, hs], axis=1, keepdims=True) * vs[:, hs]
        return carry

    lax.fori_loop(0, t, body, 0)
    b_end = _chunk_end(b_ref, reverse)
    e_cols = _decay_columns(b_end)
    q_full = (q * jnp.exp(binc)).astype(BF16)
    k_state = (kk_ref[...] * jnp.exp(b_end - binc)).astype(BF16)
    for h, hs in enumerate(heads):
        o_ref[:, hs] = (oi_ref[:, hs] + _dot(q_full[:, hs], s_ref[h].astype(BF16))).astype(o_ref.dtype)
        s_ref[h] = s_ref[h] * e_cols[:, h:h + 1] + _dot_tn(k_state[:, hs], v_ref[:, hs])


def _hgrn_kernel(flag_ref, qf_ref, vf_ref, kf_ref, gf_ref, qb_ref, vb_ref, kb_ref, gb_ref, s0f_ref, s0b_ref,
                 of_ref, ob_ref, sf_ref, sb_ref, bf_ref, bb_ref, oi_ref, kk_ref, vv_ref):
    b = pl.program_id(0)
    j = pl.program_id(1)
    n = pl.num_programs(1) * HGRN_STEP_CHUNKS
    t = HGRN_CHUNK

    @pl.when(j == 0)
    def _():
        sf_ref[...] = s0f_ref[0]
        sb_ref[...] = s0b_ref[0]

    def chunk(s, carry):
        rf = pl.ds(pl.multiple_of(s * t, t), t)
        rb = pl.ds(pl.multiple_of((HGRN_STEP_CHUNKS - 1 - s) * t, t), t)
        qf, vf, kf, gf, of = (r.at[0, rf, :] for r in (qf_ref, vf_ref, kf_ref, gf_ref, of_ref))
        qb, vb, kb, gb, ob = (r.at[0, rb, :] for r in (qb_ref, vb_ref, kb_ref, gb_ref, ob_ref))
        c = j * HGRN_STEP_CHUNKS + s
        safe = (flag_ref[(b * n + c) * 2] + flag_ref[(b * n + n - 1 - c) * 2 + 1]) == 2

        @pl.when(safe)
        def _():
            _hgrn_decay(gf, bf_ref, False)
            _hgrn_decay(gb, bb_ref, True)
            read_f, update_f = _hgrn_fast(qf, vf, kf, bf_ref, of, sf_ref, False)
            read_b, update_b = _hgrn_fast(qb, vb, kb, bb_ref, ob, sb_ref, True)
            read_f()
            read_b()
            update_f()
            update_b()

        @pl.when(jnp.logical_not(safe))
        def _():
            _hgrn_decay(gf, bf_ref, False)
            _hgrn_decay(gb, bb_ref, True)
            _hgrn_slow(qf, vf, kf, bf_ref, of, sf_ref, oi_ref, kk_ref, vv_ref, False)
            _hgrn_slow(qb, vb, kb, bb_ref, ob, sb_ref, oi_ref, kk_ref, vv_ref, True)

        return carry

    lax.fori_loop(0, HGRN_STEP_CHUNKS, chunk, 0)


def _hgrn(flags, q, v, kf, gf, kb, gb, s0f, s0b):
    b, l, w = q.shape
    t = HGRN_CHUNK
    rows_per_step = t * HGRN_STEP_CHUNKS
    n = l // rows_per_step
    fwd = pl.BlockSpec((1, rows_per_step, w), lambda i, j, f: (i, j, 0))
    bwd = pl.BlockSpec((1, rows_per_step, w), lambda i, j, f: (i, n - 1 - j, 0))
    sspec = pl.BlockSpec((1, N_HEADS, HEAD_DIM, HEAD_DIM), lambda i, j, f: (i, 0, 0, 0))
    out = jax.ShapeDtypeStruct((b, l, w), BF16)
    state = pltpu.VMEM((N_HEADS, HEAD_DIM, HEAD_DIM), F32)
    rows = pltpu.VMEM((t, w), F32)
    grid_spec = pltpu.PrefetchScalarGridSpec(
        num_scalar_prefetch=1,
        grid=(b, n),
        in_specs=[fwd, fwd, fwd, fwd, bwd, bwd, bwd, bwd, sspec, sspec],
        out_specs=[fwd, bwd],
        scratch_shapes=[state, state, rows, rows, rows, rows, rows],
    )
    return pl.pallas_call(
        _hgrn_kernel,
        grid_spec=grid_spec,
        out_shape=[out, out],
        compiler_params=_params(("arbitrary", "arbitrary")),
        name="hgrn",
    )(flags, q, v, kf, gf, q, v, kb, gb, s0f, s0b)


def _pool_col_matrices():
    mats = np.zeros((len(POOL_WINDOWS), POOL_TILE, POOL_TILE), np.float32)
    for j, w in enumerate(POOL_WINDOWS):
        for t in range(POOL_TILE):
            r, c = divmod(t, GRID_W)
            lo, hi = max(c - w // 2, 0), min(c + w // 2 - 1, GRID_W - 1)
            mats[j, t, r * GRID_W + lo:r * GRID_W + hi + 1] = 1.0
    return jnp.asarray(mats, BF16)


def _window_len(pos, half, n):
    return jnp.minimum(pos + half - 1, n - 1) + 1 - jnp.maximum(pos - half, 0)


def _pool_kernel(p_ref, a_ref, wp_ref, ps_ref, o_ref, pad_ref, inr_ref, inc_ref, *, rows):
    j = pl.program_id(1)
    half = jnp.left_shift(1, j)
    gw = GRID_W
    halo = POOL_HALO * gw
    l = rows * gw
    tile_rows = POOL_TILE // gw
    edge = POOL_HALO // 2
    for buf in range(2):
        pad_ref[buf, 0:halo, :] = jnp.zeros((halo, POOL_GROUP), F32)
        pad_ref[buf, halo + l:2 * halo + l, :] = jnp.zeros((halo, POOL_GROUP), F32)
    pad_ref[0, halo:halo + l, :] = p_ref[0]

    def level(src, dst, back, fwd):
        def body(g, carry):
            for u in range(8):
                t = pl.multiple_of((edge + g * 8 + u) * gw, gw)
                pad_ref[dst, pl.ds(t, gw), :] = (pad_ref[src, pl.ds(t - back * gw, gw), :]
                                                 + pad_ref[src, pl.ds(t + fwd * gw, gw), :])
            return carry

        lax.fori_loop(0, (rows + 2 * (POOL_HALO - edge)) // 8, body, 0)

    level(0, 1, 1, 0)

    @pl.when(j >= 1)
    def _():
        level(1, 0, 1, 1)

    @pl.when(j >= 2)
    def _():
        level(0, 1, 2, 2)

    @pl.when(j >= 3)
    def _():
        level(1, 0, 4, 4)

    summed = 1 - jnp.bitwise_and(j, 1)
    r = lax.broadcasted_iota(I32, inr_ref.shape, 0)
    inr_ref[...] = 1.0 / _window_len(r, half, rows).astype(F32)
    c = jnp.bitwise_and(lax.broadcasted_iota(I32, inc_ref.shape, 0), gw - 1)
    inc_ref[...] = 1.0 / _window_len(c, half, gw).astype(F32)
    a = a_ref[0]
    wp = wp_ref[0]
    scale = ps_ref[0]

    def step(i, carry):
        tiles = [POOL_UNROLL * i + u for u in range(POOL_UNROLL)]
        starts = [pl.multiple_of(t * POOL_TILE, POOL_TILE) for t in tiles]
        sums = []
        for t0 in starts:
            rsum = pad_ref[summed, pl.ds(halo + t0, POOL_TILE), :]
            hi = rsum.astype(BF16)
            lo = (rsum - hi.astype(F32)).astype(BF16)
            sums.append(_dot(a, jnp.concatenate([hi, lo], axis=1)))
        resid = []
        for t, t0, both in zip(tiles, starts, sums):
            total = both[:, :POOL_GROUP] + both[:, POOL_GROUP:]
            by_row = [total[rr * gw:(rr + 1) * gw] * inr_ref[pl.ds(t * tile_rows + rr, 1), :]
                      for rr in range(tile_rows)]
            resid.append((jnp.concatenate(by_row, axis=0) * inc_ref[...]
                          - p_ref[0, pl.ds(t0, POOL_TILE), :]).astype(BF16))
        for t0, rb in zip(starts, resid):
            o_ref[0, pl.ds(t0, POOL_TILE), :] = (_dot(rb, wp) * scale).astype(BF16)
        return carry

    lax.fori_loop(0, l // (POOL_UNROLL * POOL_TILE), step, 0)


def _pool(p, w_pool_bf, pool_scale):
    b, l, _ = p.shape
    rows = l // GRID_W
    ng = len(POOL_WINDOWS)
    grp = pl.BlockSpec((1, l, POOL_GROUP), lambda i, j: (i, 0, j))
    return pl.pallas_call(
        functools.partial(_pool_kernel, rows=rows),
        grid=(b, ng),
        in_specs=[grp,
                  pl.BlockSpec((1, POOL_TILE, POOL_TILE), lambda i, j: (j, 0, 0)),
                  pl.BlockSpec((1, POOL_GROUP, POOL_GROUP), lambda i, j: (j, 0, 0)),
                  pl.BlockSpec((1, 1, POOL_GROUP), lambda i, j: (j, 0, 0))],
        out_specs=grp,
        out_shape=jax.ShapeDtypeStruct((b, l, POOL_WIDTH), BF16),
        scratch_shapes=[pltpu.VMEM((2, (rows + 2 * POOL_HALO) * GRID_W, POOL_GROUP), F32),
                        pltpu.VMEM((rows, POOL_GROUP), F32),
                        pltpu.VMEM((POOL_TILE, POOL_GROUP), F32)],
        compiler_params=_params(("arbitrary", "arbitrary")),
        name="pool",
    )(p, _pool_col_matrices(), w_pool_bf, pool_scale.reshape(ng, 1, POOL_GROUP))


def _merge_kernel(x_ref, of_ref, ob_ref, og_ref, pl_ref, ga_ref, gb_ref, mod_ref, hg_ref, n2_ref,
                  wa_ref, wb_ref, wo_ref, wr_ref, nh_ref, vx_ref, pr_ref):
    b = pl.program_id(0)
    m = mod_ref[pl.ds(b, 1), :]
    g1 = m[:, 2 * D_MODEL:3 * D_MODEL]
    sh2 = m[:, 3 * D_MODEL:4 * D_MODEL]
    sc2 = m[:, 4 * D_MODEL:5 * D_MODEL]
    hg = hg_ref[...]
    for r0 in range(0, x_ref.shape[1], MERGE_ROWS):
        rs = slice(r0, r0 + MERGE_ROWS)
        gated = []
        for h in range(N_HEADS):
            hs = slice(h * HEAD_DIM, (h + 1) * HEAD_DIM)
            o = of_ref[0, rs, hs].astype(F32) + ob_ref[0, rs, hs].astype(F32)
            gated.append((_rms(o, hg) * og_ref[0, rs, hs].astype(F32)).astype(BF16))
        ya = _dot(jnp.concatenate(gated, axis=1), wa_ref[...])
        yb = _dot(pl_ref[0, rs, :], wb_ref[...])
        y = ga_ref[0, rs, :].astype(F32) * ya + gb_ref[0, rs, :].astype(F32) * yb
        nh = x_ref[0, rs, :] + g1 * _dot(y.astype(BF16), wo_ref[...])
        nh_ref[0, rs, :] = nh
        vx = (_rms(nh, n2_ref[...]) * (1.0 + sc2) + sh2).astype(BF16)
        vx_ref[0, rs, :] = vx
        logits = _dot_nt(wr_ref[...], vx)
        ex = jnp.exp(logits - jnp.max(logits, axis=0, keepdims=True))
        pr_ref[0, :, rs] = ex / jnp.sum(ex, axis=0, keepdims=True)


def _merge(x, o_f, o_b, og, pooled, ga, gb, mod, hgrn_g, norm2_g, w_a_bf, w_b_bf, w_out_bf, w_router_t):
    b, l, d = x.shape
    tm = PROJ_TILE
    tok = lambda w: pl.BlockSpec((1, tm, w), lambda i, j: (i, j, 0))
    const = lambda a: pl.BlockSpec(a.shape, lambda i, j: (0,) * a.ndim)
    return pl.pallas_call(
        _merge_kernel,
        grid=(b, l // tm),
        in_specs=[tok(d), tok(d), tok(d), tok(d), tok(POOL_WIDTH), tok(d), tok(d),
                  const(mod), const(hgrn_g), const(norm2_g),
                  _resident(w_a_bf.shape, lambda i, j: (0, 0)),
                  _resident(w_b_bf.shape, lambda i, j: (0, 0)),
                  _resident(w_out_bf.shape, lambda i, j: (0, 0)),
                  const(w_router_t)],
        out_specs=[tok(d), tok(d), pl.BlockSpec((1, N_EXPERTS, tm), lambda i, j: (i, 0, j))],
        out_shape=[jax.ShapeDtypeStruct((b, l, d), F32),
                   jax.ShapeDtypeStruct((b, l, d), BF16),
                   jax.ShapeDtypeStruct((b, N_EXPERTS, l), F32)],
        compiler_params=_params(("arbitrary", "arbitrary")),
        name="merge",
    )(x, o_f, o_b, og, pooled, ga, gb, mod, hgrn_g, norm2_g, w_a_bf, w_b_bf, w_out_bf, w_router_t)


def _route_kernel(p_ref, code_ref, start_ref, *, cap):
    l = p_ref.shape[2]
    tm = TOKEN_TILE
    bits = lax.bitcast_convert_type(p_ref[0], I32)
    capf = jnp.float32(cap)

    def count_ge(cand):
        return jnp.sum(jnp.where(bits >= cand, 1.0, 0.0), axis=1, keepdims=True)

    def search(i, prefix):
        cand = jnp.bitwise_or(prefix, jnp.left_shift(1, 30 - i))
        return jnp.where(count_ge(cand) >= capf, cand, prefix)

    tau = lax.fori_loop(0, 31, search, jnp.zeros((N_EXPERTS, 1), I32))
    need = capf - jnp.sum(jnp.where(bits > tau, 1.0, 0.0), axis=1, keepdims=True)
    before = _one_zero(lax.broadcasted_iota(I32, (tm, tm), 0) < lax.broadcasted_iota(I32, (tm, tm), 1))
    lane = lax.broadcasted_iota(I32, (N_EXPERTS, 128), 1)

    def tile(k, carry):
        n_gt, n_eq, starts = carry
        sl = pl.ds(pl.multiple_of(k * tm, tm), tm)
        bk = lax.bitcast_convert_type(p_ref[0, :, sl], I32)
        gt = bk > tau
        eq = bk == tau
        gt01 = _one_zero(gt)
        eq01 = _one_zero(eq)
        eq_before = n_eq + _dot(eq01, before)
        chosen = jnp.logical_or(gt, jnp.logical_and(eq, eq_before < need))
        rank = n_gt + _dot(gt01, before) + jnp.minimum(eq_before, need)
        code_ref[0, :, sl] = jnp.where(chosen, rank, -1.0).astype(I32)
        starts = jnp.where(lane == k, (n_gt + jnp.minimum(n_eq, need)).astype(I32), starts)
        n_gt = n_gt + jnp.sum(gt01.astype(F32), axis=1, keepdims=True)
        n_eq = n_eq + jnp.sum(eq01.astype(F32), axis=1, keepdims=True)
        return n_gt, n_eq, starts

    zero = jnp.zeros((N_EXPERTS, 1), F32)
    _, _, starts = lax.fori_loop(0, l // tm, tile, (zero, zero, jnp.zeros((N_EXPERTS, 128), I32)))
    start_ref[0] = jnp.where(lane == l // tm, cap, starts)


def _route(probs_t, cap):
    b, e, l = probs_t.shape
    assert l // TOKEN_TILE < 128
    return pl.pallas_call(
        functools.partial(_route_kernel, cap=cap),
        grid=(b,),
        in_specs=[pl.BlockSpec((1, e, l), lambda i: (i, 0, 0))],
        out_specs=[pl.BlockSpec((1, e, l), lambda i: (i, 0, 0)),
                   pl.BlockSpec((1, e, 128), lambda i: (i, 0, 0))],
        out_shape=[jax.ShapeDtypeStruct((b, e, l), I32), jax.ShapeDtypeStruct((b, e, 128), I32)],
        compiler_params=_params(("arbitrary",)),
        name="route",
    )(probs_t)


def _floor8(n):
    return jnp.left_shift(jnp.right_shift(n, 3), 3)


def _dispatch_kernel(start_ref, vx_ref, code_ref, xg_hbm, stage_ref, carry_ref, sent_ref, sem,
                     *, cap, n_tiles, n_steps):
    b = pl.program_id(0)
    k = pl.program_id(1)
    step = b * n_tiles + k
    w = SLOT_WINDOW
    tm = TOKEN_TILE
    ne = N_EXPERTS

    @pl.when(step == 0)
    def _():
        sent_ref[0] = 0

    @pl.when(k == 0)
    def _():
        carry_ref[...] = jnp.zeros(carry_ref.shape, F32)

    base, tail = [], []
    n_pass = jnp.int32(1)
    for e in range(ne):
        i0 = (b * ne + e) * 128 + k
        end = start_ref[i0 + 1]
        base.append(_floor8(start_ref[i0]))
        tail.append(_floor8(end))
        n_pass = jnp.maximum(n_pass, jnp.right_shift(end - base[e] + w - 1, SLOT_SHIFT))

    def send(buf, rows):
        @pl.when(sent_ref[0] > 0)
        def _():
            for e in range(ne):
                pltpu.make_async_copy(stage_ref.at[0, pl.ds(e * w, w), :], xg_hbm.at[b, e, pl.ds(0, w), :],
                                      sem.at[e]).wait()

        for e in range(ne):
            pltpu.make_async_copy(stage_ref.at[buf, pl.ds(e * w, w), :],
                                  xg_hbm.at[b, e, pl.ds(pl.multiple_of(rows[e], 8), w), :], sem.at[e]).start()
        sent_ref[0] = sent_ref[0] + 1

    slot = lax.broadcasted_iota(I32, (w, tm), 0)
    group = lax.broadcasted_iota(I32, (8, tm), 0)

    def one_pass(c, carry):
        buf = jnp.bitwise_and(sent_ref[0], 1)
        hits = [_one_zero(code_ref[0, e:e + 1, :] == (slot + (base[e] + c * w))) for e in range(ne)]
        tails = [_one_zero(code_ref[0, e:e + 1, :] == (group + tail[e])) for e in range(ne)]
        rows = _dot(jnp.concatenate(hits + tails, axis=0), vx_ref[0])
        stage_ref[buf] = rows[:ne * w].astype(BF16)

        @pl.when(c == 0)
        def _():
            for e in range(ne):
                first = rows[e * w:e * w + 8] + carry_ref[e]
                stage_ref[buf, e * w:e * w + 8, :] = first.astype(BF16)
                own = rows[ne * w + 8 * e:ne * w + 8 * e + 8]
                carry_ref[e] = jnp.where(tail[e] == base[e], carry_ref[e], 0.0) + own

        send(buf, [jnp.minimum(base[e] + c * w, cap) for e in range(ne)])
        return carry

    lax.fori_loop(0, n_pass, one_pass, 0)

    @pl.when(k == n_tiles - 1)
    def _():
        buf = jnp.bitwise_and(sent_ref[0], 1)
        stage_ref[buf] = jnp.zeros(stage_ref.shape[1:], BF16)
        send(buf, [cap] * ne)

    @pl.when(step == n_steps - 1)
    def _():
        for e in range(ne):
            pltpu.make_async_copy(stage_ref.at[0, pl.ds(e * w, w), :], xg_hbm.at[b, e, pl.ds(0, w), :],
                                  sem.at[e]).wait()


def _dispatch(starts_flat, vx, code, cap):
    b, l, d = vx.shape
    tm = TOKEN_TILE
    n_tiles = l // tm
    grid_spec = pltpu.PrefetchScalarGridSpec(
        num_scalar_prefetch=1,
        grid=(b, n_tiles),
        in_specs=[pl.BlockSpec((1, tm, d), lambda i, j, s: (i, j, 0)),
                  pl.BlockSpec((1, N_EXPERTS, tm), lambda i, j, s: (i, 0, j))],
        out_specs=pl.BlockSpec(memory_space=pl.ANY),
        scratch_shapes=[pltpu.VMEM((2, N_EXPERTS * SLOT_WINDOW, d), BF16),
                        pltpu.VMEM((N_EXPERTS, 8, d), F32),
                        pltpu.SMEM((1,), I32),
                        pltpu.SemaphoreType.DMA((N_EXPERTS,))],
    )
    return pl.pallas_call(
        functools.partial(_dispatch_kernel, cap=cap, n_tiles=n_tiles, n_steps=b * n_tiles),
        grid_spec=grid_spec,
        out_shape=jax.ShapeDtypeStruct((b, N_EXPERTS, cap + SLOT_WINDOW, d), BF16),
        compiler_params=_params(("arbitrary", "arbitrary")),
        name="dispatch",
    )(starts_flat, vx, code)


def _moe_kernel(xg_ref, wg_ref, wu_ref, wd_ref, y_ref, wgb_ref, wub_ref, wdb_ref, *, cap):
    @pl.when(pl.program_id(1) == 0)
    def _():
        wgb_ref[...] = wg_ref[0].astype(BF16)
        wub_ref[...] = wu_ref[0].astype(BF16)
        wdb_ref[...] = wd_ref[0].astype(BF16)

    fr = min(FFN_ROWS, cap)
    for r0 in range(0, cap, fr):
        xg = xg_ref[0, 0, r0:r0 + fr, :]
        hid = (_silu(_dot(xg, wgb_ref[...])) * _dot(xg, wub_ref[...])).astype(BF16)
        y_ref[0, 0, r0:r0 + fr, :] = _dot(hid, wdb_ref[...]).astype(BF16)


def _moe(xg, w_gate, w_up, w_down, cap):
    b, ne, _, d = xg.shape
    ff = w_gate.shape[2]
    rows = pl.BlockSpec((1, 1, cap, d), lambda e, i: (i, e, 0, 0))
    return pl.pallas_call(
        functools.partial(_moe_kernel, cap=cap),
        grid=(ne, b),
        in_specs=[rows,
                  pl.BlockSpec((1, d, ff), lambda e, i: (e, 0, 0)),
                  pl.BlockSpec((1, d, ff), lambda e, i: (e, 0, 0)),
                  pl.BlockSpec((1, ff, d), lambda e, i: (e, 0, 0))],
        out_specs=rows,
        out_shape=jax.ShapeDtypeStruct((b, ne, cap, d), BF16),
        scratch_shapes=[pltpu.VMEM((d, ff), BF16), pltpu.VMEM((d, ff), BF16), pltpu.VMEM((ff, d), BF16)],
        compiler_params=_params(("arbitrary", "arbitrary")),
        name="moe",
    )(xg, w_gate, w_up, w_down)


def _combine_kernel(start_ref, nh_ref, code_ref, p_ref, mod_ref, fg_ref, y_hbm, o_ref, ybuf, sem,
                    *, cap, n_tiles, n_steps):
    b = pl.program_id(0)
    k = pl.program_id(1)
    step = b * n_tiles + k
    tm = TOKEN_TILE
    w = SLOT_WINDOW

    def window(bb, kk, e, c):
        st = start_ref[(bb * N_EXPERTS + e) * 128 + kk]
        lo = jnp.left_shift(jnp.right_shift(st, 3), 3) + c * w
        return lo, pl.multiple_of(jnp.minimum(lo, cap - w), 8)

    def copies(bb, kk, buf):
        out = []
        for e in range(N_EXPERTS):
            _, src = window(bb, kk, e, 0)
            out.append(pltpu.make_async_copy(y_hbm.at[bb, e, pl.ds(src, w), :], ybuf.at[buf, e], sem.at[buf, e]))
        return out

    def later(ahead):
        kk = k + ahead
        wrap = kk >= n_tiles
        return jnp.where(wrap, b + 1, b), jnp.where(wrap, kk - n_tiles, kk)

    @pl.when(step == 0)
    def _():
        for a in range(COMBINE_AHEAD):
            for cp in copies(*later(a), a):
                cp.start()

    cur = lax.rem(step, COMBINE_AHEAD + 1)

    @pl.when(step + COMBINE_AHEAD < n_steps)
    def _():
        for cp in copies(*later(COMBINE_AHEAD), lax.rem(step + COMBINE_AHEAD, COMBINE_AHEAD + 1)):
            cp.start()

    for cp in copies(b, k, cur):
        cp.wait()

    slot = lax.broadcasted_iota(I32, (w, tm), 0)
    p = p_ref[0]

    def scatter(c, buf_rows):
        weights = []
        for e in range(N_EXPERTS):
            lo, src = window(b, k, e, c)
            codes = code_ref[0, e:e + 1, :]
            hit = jnp.logical_and(codes == (slot + src), jnp.logical_and(codes >= lo, codes < lo + w))
            weights.append(jnp.where(hit, p[e:e + 1, :], 0.0).astype(BF16))
        return _dot_tn(jnp.concatenate(weights, axis=0), buf_rows)

    acc = scatter(0, ybuf[cur].reshape(N_EXPERTS * w, D_MODEL))

    most = jnp.int32(0)
    for e in range(N_EXPERTS):
        i0 = (b * N_EXPERTS + e) * 128 + k
        st = start_ref[i0]
        span = st - jnp.left_shift(jnp.right_shift(st, 3), 3) + start_ref[i0 + 1] - st
        most = jnp.maximum(most, jnp.right_shift(span + w - 1, SLOT_SHIFT))

    def extra(c, acc):
        cps = []
        for e in range(N_EXPERTS):
            _, src = window(b, k, e, c)
            cps.append(pltpu.make_async_copy(y_hbm.at[b, e, pl.ds(src, w), :], ybuf.at[cur, e], sem.at[cur, e]))
        for cp in cps:
            cp.start()
        for cp in cps:
            cp.wait()
        return acc + scatter(c, ybuf[cur].reshape(N_EXPERTS * w, D_MODEL))

    acc = lax.fori_loop(1, most, extra, acc)
    g2 = mod_ref[pl.ds(b, 1), 5 * D_MODEL:6 * D_MODEL]
    o_ref[0] = _rms(nh_ref[0] + g2 * acc, fg_ref[...])


def _combine(starts_flat, new_hx, code, probs_t, mod, final_g, y, cap):
    b, l, d = new_hx.shape
    tm = TOKEN_TILE
    n_tiles = l // tm
    grid_spec = pltpu.PrefetchScalarGridSpec(
        num_scalar_prefetch=1,
        grid=(b, n_tiles),
        in_specs=[pl.BlockSpec((1, tm, d), lambda i, j, s: (i, j, 0)),
                  pl.BlockSpec((1, N_EXPERTS, tm), lambda i, j, s: (i, 0, j)),
                  pl.BlockSpec((1, N_EXPERTS, tm), lambda i, j, s: (i, 0, j)),
                  pl.BlockSpec(mod.shape, lambda i, j, s: (0, 0)),
                  pl.BlockSpec((1, d), lambda i, j, s: (0, 0)),
                  pl.BlockSpec(memory_space=pl.ANY)],
        out_specs=pl.BlockSpec((1, tm, d), lambda i, j, s: (i, j, 0)),
        scratch_shapes=[pltpu.VMEM((COMBINE_AHEAD + 1, N_EXPERTS, SLOT_WINDOW, d), BF16),
                        pltpu.SemaphoreType.DMA((COMBINE_AHEAD + 1, N_EXPERTS))],
    )
    return pl.pallas_call(
        functools.partial(_combine_kernel, cap=cap, n_tiles=n_tiles, n_steps=b * n_tiles),
        grid_spec=grid_spec,
        out_shape=jax.ShapeDtypeStruct((b, l, d), F32),
        compiler_params=_params(("arbitrary", "arbitrary")),
        name="combine",
    )(starts_flat, new_hx, code, probs_t, mod, final_g, y)


def _layer(x, c, ctx, c_ctx, lb_logits, w_mod, b_mod, norm1_g, w_in, hgrn_norm_g, w_a, w_pool, pool_scale,
           w_b, w_out, norm2_g, w_router, w_e_gate, w_e_up, w_e_down, final_g):
    b, l, d = x.shape
    cap = CAPACITY_FACTOR * l // N_EXPERTS
    assert b < MOD_ROWS and l % (HGRN_CHUNK * HGRN_STEP_CHUNKS) == 0 and l % PROJ_TILE == 0 and l % TOKEN_TILE == 0
    assert cap % 8 == 0 and cap >= SLOT_WINDOW and l % (POOL_UNROLL * POOL_TILE) == 0
    cc = jnp.zeros((MOD_ROWS, d), F32).at[:b].set(c).at[b].set(c_ctx)
    assert lb_logits.shape == (2, 2, HGRN_WIDTH)
    lbl = lb_logits.reshape(4, HGRN_WIDTH)
    row = lambda a: a.reshape(1, -1)
    w_in_bf = w_in.astype(BF16)

    mod = _adaln(cc, w_mod, b_mod)
    s0f, s0b = _ctx_states(ctx, row(norm1_g), mod, lbl, w_in_bf)
    q, v, kf, gf, kb, gb, og, p, ga, gbm, safe = _in_proj(x, row(norm1_g), mod, lbl, w_in_bf)
    per_tile = TOKEN_TILE // HGRN_CHUNK
    safe = safe[:, :, :2 * per_tile, 0].reshape(b, -1, 2, per_tile)
    flags = safe.transpose(0, 1, 3, 2).astype(I32).reshape(-1)
    o_f, o_b = _hgrn(flags, q, v, kf, gf, kb, gb, s0f, s0b)
    pooled = _pool(p, w_pool.astype(BF16), pool_scale)
    new_hx, vx, probs_t = _merge(x, o_f, o_b, og, pooled, ga, gbm, mod, row(hgrn_norm_g), row(norm2_g),
                                 w_a.astype(BF16), w_b.astype(BF16), w_out.astype(BF16),
                                 w_router.T.astype(BF16))
    code, starts = _route(probs_t, cap)
    starts_flat = starts.reshape(-1)
    xg = _dispatch(starts_flat, vx, code, cap)
    y = _moe(xg, w_e_gate, w_e_up, w_e_down, cap)
    return _combine(starts_flat, new_hx, code, probs_t, mod, row(final_g), y, cap)


def kernel(x, c, ctx, c_ctx, lb_logits, w_mod, b_mod, norm1_g, w_in, hgrn_norm_g, w_a, w_pool, pool_scale,
           w_b, w_out, norm2_g, w_router, w_e_gate, w_e_up, w_e_down, final_g):
    assert w_mod.shape[0] == 1, "single-layer trunk: the context stream only seeds the latent recurrence"
    return _layer(x, c, ctx, c_ctx, lb_logits, w_mod[0], b_mod[0], norm1_g[0], w_in[0], hgrn_norm_g[0], w_a[0],
                  w_pool[0], pool_scale[0], w_b[0], w_out[0], norm2_g[0], w_router[0], w_e_gate[0],
                  w_e_up[0], w_e_down[0], final_g)
```

```python
import functools

import numpy as np
import jax
import jax.numpy as jnp
from jax import lax
from jax.experimental import pallas as pl
from jax.experimental.pallas import tpu as pltpu

F32 = jnp.float32
BF16 = jnp.bfloat16
I32 = jnp.int32

D_MODEL = 1024
N_HEADS = 8
HEAD_DIM = 128
HGRN_WIDTH = N_HEADS * HEAD_DIM
GRID_W = 64
POOL_WINDOWS = (2, 4, 8, 16)
POOL_GROUP = 128
POOL_WIDTH = POOL_GROUP * len(POOL_WINDOWS)
N_EXPERTS = 16
CAPACITY_FACTOR = 2
EXPERT_FF = 1024
EPS = 1e-6

Q_OFF = 0
I_OFF = Q_OFF + HGRN_WIDTH
FF_OFF = I_OFF + HGRN_WIDTH
FB_OFF = FF_OFF + HGRN_WIDTH
OG_OFF = FB_OFF + HGRN_WIDTH
P_OFF = OG_OFF + HGRN_WIDTH
GA_OFF = P_OFF + POOL_WIDTH
GB_OFF = GA_OFF + D_MODEL
IN_COLS = GB_OFF + D_MODEL

LANES = 128
SUBLANES = 8
SUBLANE_SHIFT = 3
MOD_ROWS = 16
ADALN_TILE = 768
PROJ_TILE = 512
MERGE_ROWS = 512
TOKEN_TILE = 256
HGRN_CHUNK = 128
HGRN_BLOCK = 32
HGRN_STEP_CHUNKS = 4
HGRN_SAFE_DECAY = 80.0
POOL_HALO = 16
POOL_TILE = 256
POOL_UNROLL = 8
SLOT_SHIFT = 6
SLOT_WINDOW = 1 << SLOT_SHIFT
FFN_ROWS = 256
COMBINE_AHEAD = 2
VMEM_LIMIT = 56 * 1024 * 1024


def _dot(a, b):
    return jnp.dot(a, b, preferred_element_type=F32)


def _dot_nt(a, b):
    return lax.dot_general(a, b, (((1,), (1,)), ((), ())), preferred_element_type=F32)


def _dot_tn(a, b):
    return lax.dot_general(a, b, (((0,), (0,)), ((), ())), preferred_element_type=F32)


def _rms(x, g):
    ms = jnp.mean(x * x, axis=-1, keepdims=True)
    return x * lax.rsqrt(ms + EPS) * g


def _silu(z):
    return z * jax.nn.sigmoid(z)


def _lower_bound(a0, a1):
    m = jnp.maximum(a0, a1)
    e0 = jnp.exp(a0 - m)
    e1 = jnp.exp(a1 - m)
    return e0 / (e0 + e1)


def _split3(g):
    hi = g.astype(BF16)
    r = g - hi.astype(F32)
    mid = r.astype(BF16)
    lo = (r - mid.astype(F32)).astype(BF16)
    return hi, mid, lo


def _apply01(u01, g):
    hi, mid, lo = _split3(g)
    return _dot(u01, hi) + _dot(u01, mid) + _dot(u01, lo)


def _one_zero(mask):
    return jnp.where(mask, 1.0, 0.0).astype(BF16)


def _params(sem):
    return pltpu.CompilerParams(dimension_semantics=sem, vmem_limit_bytes=VMEM_LIMIT)


def _resident(shape, index_map):
    return pl.BlockSpec(shape, index_map, pipeline_mode=pl.Buffered(1))


def _adaln_kernel(c_ref, w_ref, b_ref, o_ref):
    c = c_ref[...]
    o_ref[...] = _dot(_silu(c).astype(BF16), w_ref[...].astype(BF16)) + b_ref[...]


def _adaln(cc, w_mod, b_mod):
    n = w_mod.shape[1]
    tn = ADALN_TILE
    assert n % tn == 0
    return pl.pallas_call(
        _adaln_kernel,
        grid=(n // tn,),
        in_specs=[pl.BlockSpec((MOD_ROWS, D_MODEL), lambda j: (0, 0)),
                  pl.BlockSpec((D_MODEL, tn), lambda j: (0, j)),
                  pl.BlockSpec((1, tn), lambda j: (0, j))],
        out_specs=pl.BlockSpec((MOD_ROWS, tn), lambda j: (0, j)),
        out_shape=jax.ShapeDtypeStruct((MOD_ROWS, n), F32),
        compiler_params=_params(("arbitrary",)),
        name="adaln",
    )(cc, w_mod, b_mod.reshape(1, n))


def _ctx_kernel(ctx_ref, g1_ref, mod_ref, lbl_ref, wi_ref, wf_ref, wb_ref, sf_ref, sb_ref, *, ctx_row):
    x = ctx_ref[0]
    n = x.shape[0]
    sh = mod_ref[ctx_row:ctx_row + 1, 0:D_MODEL]
    sc = mod_ref[ctx_row:ctx_row + 1, D_MODEL:2 * D_MODEL]
    u = (_rms(x, g1_ref[...]) * (1.0 + sc) + sh).astype(BF16)
    v = _dot(u, wi_ref[...]).astype(BF16)
    zf = _dot(u, wf_ref[...])
    zb = _dot(u, wb_ref[...])
    lbf = _lower_bound(lbl_ref[0:1, :], lbl_ref[1:2, :])
    lbb = _lower_bound(lbl_ref[2:3, :], lbl_ref[3:4, :])
    ff = lbf + (1.0 - lbf) * jax.nn.sigmoid(zf)
    fb = lbb + (1.0 - lbb) * jax.nn.sigmoid(zb)
    r = lax.broadcasted_iota(I32, (n, n), 0)
    c = lax.broadcasted_iota(I32, (n, n), 1)
    ef = _apply01(_one_zero(c > r), jnp.log(ff))
    eb = _apply01(_one_zero(c < r), jnp.log(fb))
    kf = ((1.0 - ff) * jnp.exp(ef)).astype(BF16)
    kb = ((1.0 - fb) * jnp.exp(eb)).astype(BF16)
    for h in range(N_HEADS):
        hs = slice(h * HEAD_DIM, (h + 1) * HEAD_DIM)
        sf_ref[0, h] = _dot_tn(kf[:, hs], v[:, hs])
        sb_ref[0, h] = _dot_tn(kb[:, hs], v[:, hs])


def _ctx_states(ctx, norm1_g, mod, lbl, w_in_bf):
    b, lc, d = ctx.shape
    col = lambda k: pl.BlockSpec((d, HGRN_WIDTH), lambda i, k=k: (0, k))
    state = jax.ShapeDtypeStruct((b, N_HEADS, HEAD_DIM, HEAD_DIM), F32)
    sspec = pl.BlockSpec((1, N_HEADS, HEAD_DIM, HEAD_DIM), lambda i: (i, 0, 0, 0))
    return pl.pallas_call(
        functools.partial(_ctx_kernel, ctx_row=b),
        grid=(b,),
        in_specs=[pl.BlockSpec((1, lc, d), lambda i: (i, 0, 0)),
                  pl.BlockSpec((1, d), lambda i: (0, 0)),
                  pl.BlockSpec(mod.shape, lambda i: (0, 0)),
                  pl.BlockSpec(lbl.shape, lambda i: (0, 0)),
                  col(I_OFF // HGRN_WIDTH), col(FF_OFF // HGRN_WIDTH), col(FB_OFF // HGRN_WIDTH)],
        out_specs=[sspec, sspec],
        out_shape=[state, state],
        compiler_params=_params(("arbitrary",)),
        name="ctx_state",
    )(ctx, norm1_g, mod, lbl, w_in_bf, w_in_bf, w_in_bf)


def _chunks_safe(logf):
    out = []
    for c0 in range(0, logf.shape[0], HGRN_CHUNK):
        worst = None
        for lo in range(c0, c0 + HGRN_CHUNK, HGRN_BLOCK):
            dec = jnp.sum(logf[lo:lo + HGRN_BLOCK], axis=0, keepdims=True)
            worst = dec if worst is None else jnp.minimum(worst, dec)
        ok = jnp.min(worst, axis=1, keepdims=True) >= -HGRN_SAFE_DECAY
        out.append(jnp.broadcast_to(jnp.where(ok, 1.0, 0.0), (1, LANES)))
    return out


def _inproj_kernel(x_ref, g1_ref, mod_ref, lbl_ref, w_ref,
                   q_ref, v_ref, kf_ref, gf_ref, kb_ref, gb_ref, og_ref, p_ref, ga_ref, gbm_ref, safe_ref):
    b = pl.program_id(0)
    x = x_ref[0]
    m = mod_ref[pl.ds(b, 1), :]
    sh = m[:, 0:D_MODEL]
    sc = m[:, D_MODEL:2 * D_MODEL]
    u = (_rms(x, g1_ref[...]) * (1.0 + sc) + sh).astype(BF16)

    z = _dot(u, w_ref[:, Q_OFF:I_OFF])
    q_ref[0] = _silu(z).astype(BF16)

    lbf = _lower_bound(lbl_ref[0:1, :], lbl_ref[1:2, :])
    f = lbf + (1.0 - lbf) * jax.nn.sigmoid(_dot(u, w_ref[:, FF_OFF:FB_OFF]))
    kf_ref[0] = (1.0 - f).astype(BF16)
    logf = jnp.log(f)
    gf_ref[0] = logf
    flags = _chunks_safe(logf)
    lbb = _lower_bound(lbl_ref[2:3, :], lbl_ref[3:4, :])
    f = lbb + (1.0 - lbb) * jax.nn.sigmoid(_dot(u, w_ref[:, FB_OFF:OG_OFF]))
    kb_ref[0] = (1.0 - f).astype(BF16)
    logf = jnp.log(f)
    gb_ref[0] = logf
    flags = flags + _chunks_safe(logf)
    safe_ref[0, 0] = jnp.concatenate(flags + [jnp.zeros((SUBLANES - len(flags), LANES), F32)], axis=0)

    z = _dot(u, w_ref[:, OG_OFF:P_OFF])
    og_ref[0] = _silu(z).astype(BF16)
    ga_ref[0] = jax.nn.sigmoid(_dot(u, w_ref[:, GA_OFF:GB_OFF])).astype(BF16)
    gbm_ref[0] = jax.nn.sigmoid(_dot(u, w_ref[:, GB_OFF:IN_COLS])).astype(BF16)
    p_ref[0] = _dot(u, w_ref[:, P_OFF:GA_OFF])
    v_ref[0] = _dot(u, w_ref[:, I_OFF:FF_OFF]).astype(BF16)


def _in_proj(x, norm1_g, mod, lbl, w_in_bf):
    b, l, d = x.shape
    tm = TOKEN_TILE
    tok = lambda w: pl.BlockSpec((1, tm, w), lambda i, j: (i, j, 0))
    wide = lambda dt, w=HGRN_WIDTH: jax.ShapeDtypeStruct((b, l, w), dt)
    return pl.pallas_call(
        _inproj_kernel,
        grid=(b, l // tm),
        in_specs=[tok(d),
                  pl.BlockSpec((1, d), lambda i, j: (0, 0)),
                  pl.BlockSpec(mod.shape, lambda i, j: (0, 0)),
                  pl.BlockSpec(lbl.shape, lambda i, j: (0, 0)),
                  _resident((d, IN_COLS), lambda i, j: (0, 0))],
        out_specs=[tok(HGRN_WIDTH)] * 7 + [tok(POOL_WIDTH), tok(d), tok(d),
                                            pl.BlockSpec((1, 1, SUBLANES, LANES), lambda i, j: (i, j, 0, 0))],
        out_shape=[wide(BF16), wide(BF16), wide(BF16), wide(F32), wide(BF16), wide(F32), wide(BF16),
                   wide(F32, POOL_WIDTH), wide(BF16, d), wide(BF16, d),
                   jax.ShapeDtypeStruct((b, l // tm, SUBLANES, LANES), F32)],
        compiler_params=_params(("arbitrary", "arbitrary")),
        name="in_proj",
    )(x, norm1_g, mod, lbl, w_in_bf)


def _reads(i, j, reverse):
    return i <= j if reverse else i >= j


def _block_edges(b_ref, reverse):
    nb = HGRN_CHUNK // HGRN_BLOCK
    zero = jnp.zeros((1, HGRN_WIDTH), F32)
    out = []
    for j in range(nb):
        if reverse:
            r = (j + 1) * HGRN_BLOCK
            out.append(zero if j == nb - 1 else b_ref[r:r + 1, :])
        else:
            r = j * HGRN_BLOCK
            out.append(zero if j == 0 else b_ref[r - 1:r, :])
    return out


def _seen(reverse):
    t = HGRN_CHUNK
    row = lax.broadcasted_iota(I32, (t, t), 0)
    col = lax.broadcasted_iota(I32, (t, t), 1)
    return (col >= row) if reverse else (col <= row)


def _hgrn_decay(g_ref, b_ref, reverse):
    g = g_ref[...]
    hi = g.astype(BF16)
    lo = (g - hi.astype(F32)).astype(BF16)
    u01 = _one_zero(_seen(reverse))
    b_ref[...] = _dot(u01, hi) + _dot(u01, lo)


def _chunk_end(b_ref, reverse):
    return b_ref[0:1, :] if reverse else b_ref[HGRN_CHUNK - 1:HGRN_CHUNK, :]


def _decay_columns(b_end):
    e_end = jnp.exp(b_end)
    rows = [e_end[:, h * HEAD_DIM:(h + 1) * HEAD_DIM] for h in range(N_HEADS)]
    return jnp.concatenate(rows + [jnp.zeros((HEAD_DIM - N_HEADS, HEAD_DIM), F32)], axis=0).T


def _hgrn_fast(q_ref, v_ref, k_ref, b_ref, o_ref, s_ref, reverse):
    t, bl = HGRN_CHUNK, HGRN_BLOCK
    nb = t // bl
    seen = _seen(reverse)
    edges = _block_edges(b_ref, reverse)
    gain = {(i, j): jnp.exp(edges[i] - edges[j])
            for i in range(nb) for j in range(nb) if i != j and _reads(i, j, reverse)}
    b_end = _chunk_end(b_ref, reverse)
    e_cols = _decay_columns(b_end)
    zblk = jnp.zeros((bl, HEAD_DIM), BF16)
    full = nb - 1 if reverse else 0
    heads = [slice(h * HEAD_DIM, (h + 1) * HEAD_DIM) for h in range(N_HEADS)]

    scores, q_full, k_state = [], [], []
    for hs in heads:
        binc = b_ref[:, hs]
        q = q_ref[:, hs].astype(F32)
        k = k_ref[:, hs].astype(F32)
        q_own, k_own = [], []
        for i in range(nb):
            bs = slice(i * bl, (i + 1) * bl)
            q_own.append(q[bs] * jnp.exp(binc[bs] - edges[i][:, hs]))
            k_own.append((k[bs] * jnp.exp(edges[i][:, hs] - binc[bs])).astype(BF16))
        q_slabs = []
        for j in range(nb):
            parts = []
            for i in range(nb):
                if not _reads(i, j, reverse):
                    parts.append(zblk)
                elif i == j:
                    parts.append(q_own[i].astype(BF16))
                else:
                    parts.append((q_own[i] * gain[i, j][:, hs]).astype(BF16))
            q_slabs.append(jnp.concatenate(parts, axis=0))
        k_slabs = [jnp.concatenate([k_own[i] if i == j else zblk for i in range(nb)], axis=0)
                   for j in range(nb)]
        near = slice(nb // 2, nb) if reverse else slice(0, nb // 2)
        near_rows = slice(t // 2, t) if reverse else slice(0, t // 2)
        far_rows = slice(0, t // 2) if reverse else slice(t // 2, t)
        s_near = _dot_nt(jnp.concatenate([s[near_rows] for s in q_slabs[near]], axis=1),
                         jnp.concatenate(k_slabs[near], axis=1))
        s_far = _dot_nt(jnp.concatenate([s[far_rows] for s in q_slabs], axis=1),
                        jnp.concatenate(k_slabs, axis=1))
        scores.append(jnp.concatenate([s_far, s_near] if reverse else [s_near, s_far], axis=0))
        q_full.append(q_slabs[full])
        k_state.append((k * jnp.exp(b_end[:, hs] - binc)).astype(BF16))

    def readout():
        for h, hs in enumerate(heads):
            p = jnp.where(seen, scores[h], 0.0).astype(BF16)
            lhs = jnp.concatenate([p, q_full[h]], axis=1)
            rhs = jnp.concatenate([v_ref[:, hs], s_ref[h].astype(BF16)], axis=0)
            o_ref[:, hs] = _dot(lhs, rhs).astype(o_ref.dtype)

    def update():
        for h, hs in enumerate(heads):
            s_ref[h] = s_ref[h] * e_cols[:, h:h + 1] + _dot_tn(k_state[h], v_ref[:, hs])

    return readout, update


def _hgrn_slow(q_ref, v_ref, k_ref, b_ref, o_ref, s_ref, oi_ref, kk_ref, vv_ref, reverse):
    t = HGRN_CHUNK
    q = q_ref[...].astype(F32)
    binc = b_ref[...]
    kk_ref[...] = k_ref[...].astype(F32)
    vv_ref[...] = v_ref[...].astype(F32)
    oi_ref[...] = jnp.zeros((t, HGRN_WIDTH), F32)
    trow = lax.broadcasted_iota(I32, (t, HGRN_WIDTH), 0)
    heads = [slice(h * HEAD_DIM, (h + 1) * HEAD_DIM) for h in range(N_HEADS)]

    def body(s, carry):
        reads = (trow <= s) if reverse else (trow >= s)
        w = jnp.exp(jnp.where(reads, binc - b_ref[pl.ds(s, 1), :], 0.0))
        prod = jnp.where(reads, q * kk_ref[pl.ds(s, 1), :] * w, 0.0)
        vs = vv_ref[pl.ds(s, 1), :]
        for hs in heads:
            oi_ref[:, hs] += jnp.sum(prod[:, hs], axis=1, keepdims=True) * vs[:, hs]
        return carry

    lax.fori_loop(0, t, body, 0)
    b_end = _chunk_end(b_ref, reverse)
    e_cols = _decay_columns(b_end)
    q_full = (q * jnp.exp(binc)).astype(BF16)
    k_state = (kk_ref[...] * jnp.exp(b_end - binc)).astype(BF16)
    for h, hs in enumerate(heads):
        o_ref[:, hs] = (oi_ref[:, hs] + _dot(q_full[:, hs], s_ref[h].astype(BF16))).astype(o_ref.dtype)
        s_ref[h] = s_ref[h] * e_cols[:, h:h + 1] + _dot_tn(k_state[:, hs], v_ref[:, hs])


def _hgrn_kernel(flag_ref, qf_ref, vf_ref, kf_ref, gf_ref, qb_ref, vb_ref, kb_ref, gb_ref, s0f_ref, s0b_ref,
                 of_ref, ob_ref, sf_ref, sb_ref, bf_ref, bb_ref, oi_ref, kk_ref, vv_ref):
    b = pl.program_id(0)
    j = pl.program_id(1)
    n = pl.num_programs(1) * HGRN_STEP_CHUNKS
    t = HGRN_CHUNK

    @pl.when(j == 0)
    def _():
        sf_ref[...] = s0f_ref[0]
        sb_ref[...] = s0b_ref[0]

    def chunk(s, carry):
        rf = pl.ds(pl.multiple_of(s * t, t), t)
        rb = pl.ds(pl.multiple_of((HGRN_STEP_CHUNKS - 1 - s) * t, t), t)
        qf, vf, kf, gf, of = (r.at[0, rf, :] for r in (qf_ref, vf_ref, kf_ref, gf_ref, of_ref))
        qb, vb, kb, gb, ob = (r.at[0, rb, :] for r in (qb_ref, vb_ref, kb_ref, gb_ref, ob_ref))
        c = j * HGRN_STEP_CHUNKS + s
        safe = (flag_ref[(b * n + c) * 2] + flag_ref[(b * n + n - 1 - c) * 2 + 1]) == 2

        @pl.when(safe)
        def _():
            _hgrn_decay(gf, bf_ref, False)
            _hgrn_decay(gb, bb_ref, True)
            read_f, update_f = _hgrn_fast(qf, vf, kf, bf_ref, of, sf_ref, False)
            read_b, update_b = _hgrn_fast(qb, vb, kb, bb_ref, ob, sb_ref, True)
            read_f()
            read_b()
            update_f()
            update_b()

        @pl.when(jnp.logical_not(safe))
        def _():
            _hgrn_decay(gf, bf_ref, False)
            _hgrn_decay(gb, bb_ref, True)
            _hgrn_slow(qf, vf, kf, bf_ref, of, sf_ref, oi_ref, kk_ref, vv_ref, False)
            _hgrn_slow(qb, vb, kb, bb_ref, ob, sb_ref, oi_ref, kk_ref, vv_ref, True)

        return carry

    lax.fori_loop(0, HGRN_STEP_CHUNKS, chunk, 0)


def _hgrn(flags, q, v, kf, gf, kb, gb, s0f, s0b):
    b, l, w = q.shape
    t = HGRN_CHUNK
    rows_per_step = t * HGRN_STEP_CHUNKS
    n = l // rows_per_step
    fwd = pl.BlockSpec((1, rows_per_step, w), lambda i, j, f: (i, j, 0))
    bwd = pl.BlockSpec((1, rows_per_step, w), lambda i, j, f: (i, n - 1 - j, 0))
    sspec = pl.BlockSpec((1, N_HEADS, HEAD_DIM, HEAD_DIM), lambda i, j, f: (i, 0, 0, 0))
    out = jax.ShapeDtypeStruct((b, l, w), BF16)
    state = pltpu.VMEM((N_HEADS, HEAD_DIM, HEAD_DIM), F32)
    rows = pltpu.VMEM((t, w), F32)
    grid_spec = pltpu.PrefetchScalarGridSpec(
        num_scalar_prefetch=1,
        grid=(b, n),
        in_specs=[fwd, fwd, fwd, fwd, bwd, bwd, bwd, bwd, sspec, sspec],
        out_specs=[fwd, bwd],
        scratch_shapes=[state, state, rows, rows, rows, rows, rows],
    )
    return pl.pallas_call(
        _hgrn_kernel,
        grid_spec=grid_spec,
        out_shape=[out, out],
        compiler_params=_params(("arbitrary", "arbitrary")),
        name="hgrn",
    )(flags, q, v, kf, gf, q, v, kb, gb, s0f, s0b)


def _pool_col_matrices():
    mats = np.zeros((len(POOL_WINDOWS), POOL_TILE, POOL_TILE), np.float32)
    for j, w in enumerate(POOL_WINDOWS):
        for t in range(POOL_TILE):
            r, c = divmod(t, GRID_W)
            lo, hi = max(c - w // 2, 0), min(c + w // 2 - 1, GRID_W - 1)
            mats[j, t, r * GRID_W + lo:r * GRID_W + hi + 1] = 1.0
    return jnp.asarray(mats, BF16)


def _window_len(pos, half, n):
    return jnp.minimum(pos + half - 1, n - 1) + 1 - jnp.maximum(pos - half, 0)


def _pool_kernel(p_ref, a_ref, wp_ref, ps_ref, o_ref, pad_ref, inr_ref, inc_ref, *, rows):
    j = pl.program_id(1)
    half = jnp.left_shift(1, j)
    gw = GRID_W
    halo = POOL_HALO * gw
    l = rows * gw
    tile_rows = POOL_TILE // gw
    edge = POOL_HALO // 2
    for buf in range(2):
        pad_ref[buf, 0:halo, :] = jnp.zeros((halo, POOL_GROUP), F32)
        pad_ref[buf, halo + l:2 * halo + l, :] = jnp.zeros((halo, POOL_GROUP), F32)
    pad_ref[0, halo:halo + l, :] = p_ref[0]

    def level(src, dst, back, fwd):
        def body(g, carry):
            for u in range(8):
                t = pl.multiple_of((edge + g * 8 + u) * gw, gw)
                pad_ref[dst, pl.ds(t, gw), :] = (pad_ref[src, pl.ds(t - back * gw, gw), :]
                                                 + pad_ref[src, pl.ds(t + fwd * gw, gw), :])
            return carry

        lax.fori_loop(0, (rows + 2 * (POOL_HALO - edge)) // 8, body, 0)

    level(0, 1, 1, 0)

    @pl.when(j >= 1)
    def _():
        level(1, 0, 1, 1)

    @pl.when(j >= 2)
    def _():
        level(0, 1, 2, 2)

    @pl.when(j >= 3)
    def _():
        level(1, 0, 4, 4)

    summed = 1 - jnp.bitwise_and(j, 1)
    r = lax.broadcasted_iota(I32, inr_ref.shape, 0)
    inr_ref[...] = 1.0 / _window_len(r, half, rows).astype(F32)
    c = jnp.bitwise_and(lax.broadcasted_iota(I32, inc_ref.shape, 0), gw - 1)
    inc_ref[...] = 1.0 / _window_len(c, half, gw).astype(F32)
    a = a_ref[0]
    wp = wp_ref[0]
    scale = ps_ref[0]

    def step(i, carry):
        tiles = [POOL_UNROLL * i + u for u in range(POOL_UNROLL)]
        starts = [pl.multiple_of(t * POOL_TILE, POOL_TILE) for t in tiles]
        sums = []
        for t0 in starts:
            rsum = pad_ref[summed, pl.ds(halo + t0, POOL_TILE), :]
            hi = rsum.astype(BF16)
            lo = (rsum - hi.astype(F32)).astype(BF16)
            sums.append(_dot(a, jnp.concatenate([hi, lo], axis=1)))
        resid = []
        for t, t0, both in zip(tiles, starts, sums):
            total = both[:, :POOL_GROUP] + both[:, POOL_GROUP:]
            by_row = [total[rr * gw:(rr + 1) * gw] * inr_ref[pl.ds(t * tile_rows + rr, 1), :]
                      for rr in range(tile_rows)]
            resid.append((jnp.concatenate(by_row, axis=0) * inc_ref[...]
                          - p_ref[0, pl.ds(t0, POOL_TILE), :]).astype(BF16))
        for t0, rb in zip(starts, resid):
            o_ref[0, pl.ds(t0, POOL_TILE), :] = (_dot(rb, wp) * scale).astype(BF16)
        return carry

    lax.fori_loop(0, l // (POOL_UNROLL * POOL_TILE), step, 0)


def _pool(p, w_pool_bf, pool_scale):
    b, l, _ = p.shape
    rows = l // GRID_W
    ng = len(POOL_WINDOWS)
    grp = pl.BlockSpec((1, l, POOL_GROUP), lambda i, j: (i, 0, j))
    return pl.pallas_call(
        functools.partial(_pool_kernel, rows=rows),
        grid=(b, ng),
        in_specs=[grp,
                  pl.BlockSpec((1, POOL_TILE, POOL_TILE), lambda i, j: (j, 0, 0)),
                  pl.BlockSpec((1, POOL_GROUP, POOL_GROUP), lambda i, j: (j, 0, 0)),
                  pl.BlockSpec((1, 1, POOL_GROUP), lambda i, j: (j, 0, 0))],
        out_specs=grp,
        out_shape=jax.ShapeDtypeStruct((b, l, POOL_WIDTH), BF16),
        scratch_shapes=[pltpu.VMEM((2, (rows + 2 * POOL_HALO) * GRID_W, POOL_GROUP), F32),
                        pltpu.VMEM((rows, POOL_GROUP), F32),
                        pltpu.VMEM((POOL_TILE, POOL_GROUP), F32)],
        compiler_params=_params(("arbitrary", "arbitrary")),
        name="pool",
    )(p, _pool_col_matrices(), w_pool_bf, pool_scale.reshape(ng, 1, POOL_GROUP))


def _merge_kernel(x_ref, of_ref, ob_ref, og_ref, pl_ref, ga_ref, gb_ref, mod_ref, hg_ref, n2_ref,
                  wa_ref, wb_ref, wo_ref, wr_ref, nh_ref, vx_ref, pr_ref):
    b = pl.program_id(0)
    m = mod_ref[pl.ds(b, 1), :]
    g1 = m[:, 2 * D_MODEL:3 * D_MODEL]
    sh2 = m[:, 3 * D_MODEL:4 * D_MODEL]
    sc2 = m[:, 4 * D_MODEL:5 * D_MODEL]
    hg = hg_ref[...]
    for r0 in range(0, x_ref.shape[1], MERGE_ROWS):
        rs = slice(r0, r0 + MERGE_ROWS)
        gated = []
        for h in range(N_HEADS):
            hs = slice(h * HEAD_DIM, (h + 1) * HEAD_DIM)
            o = of_ref[0, rs, hs].astype(F32) + ob_ref[0, rs, hs].astype(F32)
            gated.append((_rms(o, hg) * og_ref[0, rs, hs].astype(F32)).astype(BF16))
        ya = _dot(jnp.concatenate(gated, axis=1), wa_ref[...])
        yb = _dot(pl_ref[0, rs, :], wb_ref[...])
        y = ga_ref[0, rs, :].astype(F32) * ya + gb_ref[0, rs, :].astype(F32) * yb
        nh = x_ref[0, rs, :] + g1 * _dot(y.astype(BF16), wo_ref[...])
        nh_ref[0, rs, :] = nh
        vx = (_rms(nh, n2_ref[...]) * (1.0 + sc2) + sh2).astype(BF16)
        vx_ref[0, rs, :] = vx
        logits = _dot_nt(wr_ref[...], vx)
        ex = jnp.exp(logits - jnp.max(logits, axis=0, keepdims=True))
        pr_ref[0, :, rs] = ex / jnp.sum(ex, axis=0, keepdims=True)


def _merge(x, o_f, o_b, og, pooled, ga, gb, mod, hgrn_g, norm2_g, w_a_bf, w_b_bf, w_out_bf, w_router_t):
    b, l, d = x.shape
    tm = PROJ_TILE
    tok = lambda w: pl.BlockSpec((1, tm, w), lambda i, j: (i, j, 0))
    const = lambda a: pl.BlockSpec(a.shape, lambda i, j: (0,) * a.ndim)
    return pl.pallas_call(
        _merge_kernel,
        grid=(b, l // tm),
        in_specs=[tok(d), tok(d), tok(d), tok(d), tok(POOL_WIDTH), tok(d), tok(d),
                  const(mod), const(hgrn_g), const(norm2_g),
                  _resident(w_a_bf.shape, lambda i, j: (0, 0)),
                  _resident(w_b_bf.shape, lambda i, j: (0, 0)),
                  _resident(w_out_bf.shape, lambda i, j: (0, 0)),
                  const(w_router_t)],
        out_specs=[tok(d), tok(d), pl.BlockSpec((1, N_EXPERTS, tm), lambda i, j: (i, 0, j))],
        out_shape=[jax.ShapeDtypeStruct((b, l, d), F32),
                   jax.ShapeDtypeStruct((b, l, d), BF16),
                   jax.ShapeDtypeStruct((b, N_EXPERTS, l), F32)],
        compiler_params=_params(("arbitrary", "arbitrary")),
        name="merge",
    )(x, o_f, o_b, og, pooled, ga, gb, mod, hgrn_g, norm2_g, w_a_bf, w_b_bf, w_out_bf, w_router_t)


def _route_kernel(p_ref, code_ref, start_ref, *, cap):
    l = p_ref.shape[2]
    tm = TOKEN_TILE
    bits = lax.bitcast_convert_type(p_ref[0], I32)
    capf = jnp.float32(cap)

    def count_ge(cand):
        return jnp.sum(jnp.where(bits >= cand, 1.0, 0.0), axis=1, keepdims=True)

    def search(i, prefix):
        cand = jnp.bitwise_or(prefix, jnp.left_shift(1, 30 - i))
        return jnp.where(count_ge(cand) >= capf, cand, prefix)

    tau = lax.fori_loop(0, 31, search, jnp.zeros((N_EXPERTS, 1), I32))
    need = capf - jnp.sum(jnp.where(bits > tau, 1.0, 0.0), axis=1, keepdims=True)
    before = _one_zero(lax.broadcasted_iota(I32, (tm, tm), 0) < lax.broadcasted_iota(I32, (tm, tm), 1))
    lane = lax.broadcasted_iota(I32, (N_EXPERTS, LANES), 1)

    def tile(k, carry):
        n_gt, n_eq, starts = carry
        sl = pl.ds(pl.multiple_of(k * tm, tm), tm)
        bk = lax.bitcast_convert_type(p_ref[0, :, sl], I32)
        gt = bk > tau
        eq = bk == tau
        gt01 = _one_zero(gt)
        eq01 = _one_zero(eq)
        eq_before = n_eq + _dot(eq01, before)
        chosen = jnp.logical_or(gt, jnp.logical_and(eq, eq_before < need))
        rank = n_gt + _dot(gt01, before) + jnp.minimum(eq_before, need)
        code_ref[0, :, sl] = jnp.where(chosen, rank, -1.0).astype(I32)
        starts = jnp.where(lane == k, (n_gt + jnp.minimum(n_eq, need)).astype(I32), starts)
        n_gt = n_gt + jnp.sum(gt01.astype(F32), axis=1, keepdims=True)
        n_eq = n_eq + jnp.sum(eq01.astype(F32), axis=1, keepdims=True)
        return n_gt, n_eq, starts

    zero = jnp.zeros((N_EXPERTS, 1), F32)
    _, _, starts = lax.fori_loop(0, l // tm, tile, (zero, zero, jnp.zeros((N_EXPERTS, LANES), I32)))
    start_ref[0] = jnp.where(lane == l // tm, cap, starts)


def _route(probs_t, cap):
    b, e, l = probs_t.shape
    assert l // TOKEN_TILE < LANES
    return pl.pallas_call(
        functools.partial(_route_kernel, cap=cap),
        grid=(b,),
        in_specs=[pl.BlockSpec((1, e, l), lambda i: (i, 0, 0))],
        out_specs=[pl.BlockSpec((1, e, l), lambda i: (i, 0, 0)),
                   pl.BlockSpec((1, e, LANES), lambda i: (i, 0, 0))],
        out_shape=[jax.ShapeDtypeStruct((b, e, l), I32), jax.ShapeDtypeStruct((b, e, LANES), I32)],
        compiler_params=_params(("arbitrary",)),
        name="route",
    )(probs_t)


def _floor8(n):
    return jnp.left_shift(jnp.right_shift(n, SUBLANE_SHIFT), SUBLANE_SHIFT)


def _dispatch_kernel(start_ref, vx_ref, code_ref, xg_hbm, stage_ref, carry_ref, sent_ref, sem,
                     *, cap, n_tiles, n_steps):
    b = pl.program_id(0)
    k = pl.program_id(1)
    step = b * n_tiles + k
    w = SLOT_WINDOW
    tm = TOKEN_TILE
    ne = N_EXPERTS

    @pl.when(step == 0)
    def _():
        sent_ref[0] = 0

    @pl.when(k == 0)
    def _():
        carry_ref[...] = jnp.zeros(carry_ref.shape, F32)

    base, tail = [], []
    n_pass = jnp.int32(1)
    for e in range(ne):
        i0 = (b * ne + e) * LANES + k
        end = start_ref[i0 + 1]
        base.append(_floor8(start_ref[i0]))
        tail.append(_floor8(end))
        n_pass = jnp.maximum(n_pass, jnp.right_shift(end - base[e] + w - 1, SLOT_SHIFT))

    def send(buf, rows):
        @pl.when(sent_ref[0] > 0)
        def _():
            for e in range(ne):
                pltpu.make_async_copy(stage_ref.at[0, pl.ds(e * w, w), :], xg_hbm.at[b, e, pl.ds(0, w), :],
                                      sem.at[e]).wait()

        for e in range(ne):
            pltpu.make_async_copy(stage_ref.at[buf, pl.ds(e * w, w), :],
                                  xg_hbm.at[b, e, pl.ds(pl.multiple_of(rows[e], SUBLANES), w), :],
                                  sem.at[e]).start()
        sent_ref[0] = sent_ref[0] + 1

    slot = lax.broadcasted_iota(I32, (w, tm), 0)
    g8 = SUBLANES
    group = lax.broadcasted_iota(I32, (g8, tm), 0)

    def one_pass(c, carry):
        buf = jnp.bitwise_and(sent_ref[0], 1)
        hits = [_one_zero(code_ref[0, e:e + 1, :] == (slot + (base[e] + c * w))) for e in range(ne)]
        tails = [_one_zero(code_ref[0, e:e + 1, :] == (group + tail[e])) for e in range(ne)]
        rows = _dot(jnp.concatenate(hits + tails, axis=0), vx_ref[0])
        stage_ref[buf] = rows[:ne * w].astype(BF16)

        @pl.when(c == 0)
        def _():
            for e in range(ne):
                first = rows[e * w:e * w + g8] + carry_ref[e]
                stage_ref[buf, e * w:e * w + g8, :] = first.astype(BF16)
                own = rows[ne * w + g8 * e:ne * w + g8 * (e + 1)]
                carry_ref[e] = jnp.where(tail[e] == base[e], carry_ref[e], 0.0) + own

        send(buf, [jnp.minimum(base[e] + c * w, cap) for e in range(ne)])
        return carry

    lax.fori_loop(0, n_pass, one_pass, 0)

    @pl.when(k == n_tiles - 1)
    def _():
        buf = jnp.bitwise_and(sent_ref[0], 1)
        stage_ref[buf] = jnp.zeros(stage_ref.shape[1:], BF16)
        send(buf, [cap] * ne)

    @pl.when(step == n_steps - 1)
    def _():
        for e in range(ne):
            pltpu.make_async_copy(stage_ref.at[0, pl.ds(e * w, w), :], xg_hbm.at[b, e, pl.ds(0, w), :],
                                  sem.at[e]).wait()


def _dispatch(starts_flat, vx, code, cap):
    b, l, d = vx.shape
    tm = TOKEN_TILE
    n_tiles = l // tm
    grid_spec = pltpu.PrefetchScalarGridSpec(
        num_scalar_prefetch=1,
        grid=(b, n_tiles),
        in_specs=[pl.BlockSpec((1, tm, d), lambda i, j, s: (i, j, 0)),
                  pl.BlockSpec((1, N_EXPERTS, tm), lambda i, j, s: (i, 0, j))],
        out_specs=pl.BlockSpec(memory_space=pl.ANY),
        scratch_shapes=[pltpu.VMEM((2, N_EXPERTS * SLOT_WINDOW, d), BF16),
                        pltpu.VMEM((N_EXPERTS, SUBLANES, d), F32),
                        pltpu.SMEM((1,), I32),
                        pltpu.SemaphoreType.DMA((N_EXPERTS,))],
    )
    return pl.pallas_call(
        functools.partial(_dispatch_kernel, cap=cap, n_tiles=n_tiles, n_steps=b * n_tiles),
        grid_spec=grid_spec,
        out_shape=jax.ShapeDtypeStruct((b, N_EXPERTS, cap + SLOT_WINDOW, d), BF16),
        compiler_params=_params(("arbitrary", "arbitrary")),
        name="dispatch",
    )(starts_flat, vx, code)


def _moe_kernel(xg_ref, wg_ref, wu_ref, wd_ref, y_ref, wgb_ref, wub_ref, wdb_ref, *, cap):
    @pl.when(pl.program_id(1) == 0)
    def _():
        wgb_ref[...] = wg_ref[0].astype(BF16)
        wub_ref[...] = wu_ref[0].astype(BF16)
        wdb_ref[...] = wd_ref[0].astype(BF16)

    fr = min(FFN_ROWS, cap)
    for r0 in range(0, cap, fr):
        xg = xg_ref[0, 0, r0:r0 + fr, :]
        hid = (_silu(_dot(xg, wgb_ref[...])) * _dot(xg, wub_ref[...])).astype(BF16)
        y_ref[0, 0, r0:r0 + fr, :] = _dot(hid, wdb_ref[...]).astype(BF16)


def _moe(xg, w_gate, w_up, w_down, cap):
    b, ne, _, d = xg.shape
    ff = w_gate.shape[2]
    rows = pl.BlockSpec((1, 1, cap, d), lambda e, i: (i, e, 0, 0))
    return pl.pallas_call(
        functools.partial(_moe_kernel, cap=cap),
        grid=(ne, b),
        in_specs=[rows,
                  pl.BlockSpec((1, d, ff), lambda e, i: (e, 0, 0)),
                  pl.BlockSpec((1, d, ff), lambda e, i: (e, 0, 0)),
                  pl.BlockSpec((1, ff, d), lambda e, i: (e, 0, 0))],
        out_specs=rows,
        out_shape=jax.ShapeDtypeStruct((b, ne, cap, d), BF16),
        scratch_shapes=[pltpu.VMEM((d, ff), BF16), pltpu.VMEM((d, ff), BF16), pltpu.VMEM((ff, d), BF16)],
        compiler_params=_params(("arbitrary", "arbitrary")),
        name="moe",
    )(xg, w_gate, w_up, w_down)


def _combine_kernel(start_ref, nh_ref, code_ref, p_ref, mod_ref, fg_ref, y_hbm, o_ref, ybuf, sem,
                    *, cap, n_tiles, n_steps):
    b = pl.program_id(0)
    k = pl.program_id(1)
    step = b * n_tiles + k
    tm = TOKEN_TILE
    w = SLOT_WINDOW

    def window(bb, kk, e, c):
        st = start_ref[(bb * N_EXPERTS + e) * LANES + kk]
        lo = _floor8(st) + c * w
        return lo, pl.multiple_of(jnp.minimum(lo, cap - w), SUBLANES)

    def copies(bb, kk, buf):
        out = []
        for e in range(N_EXPERTS):
            _, src = window(bb, kk, e, 0)
            out.append(pltpu.make_async_copy(y_hbm.at[bb, e, pl.ds(src, w), :], ybuf.at[buf, e], sem.at[buf, e]))
        return out

    def later(ahead):
        kk = k + ahead
        wrap = kk >= n_tiles
        return jnp.where(wrap, b + 1, b), jnp.where(wrap, kk - n_tiles, kk)

    @pl.when(step == 0)
    def _():
        for a in range(COMBINE_AHEAD):
            for cp in copies(*later(a), a):
                cp.start()

    cur = lax.rem(step, COMBINE_AHEAD + 1)

    @pl.when(step + COMBINE_AHEAD < n_steps)
    def _():
        for cp in copies(*later(COMBINE_AHEAD), lax.rem(step + COMBINE_AHEAD, COMBINE_AHEAD + 1)):
            cp.start()

    for cp in copies(b, k, cur):
        cp.wait()

    slot = lax.broadcasted_iota(I32, (w, tm), 0)
    p = p_ref[0]

    def scatter(c, buf_rows):
        weights = []
        for e in range(N_EXPERTS):
            lo, src = window(b, k, e, c)
            codes = code_ref[0, e:e + 1, :]
            hit = jnp.logical_and(codes == (slot + src), jnp.logical_and(codes >= lo, codes < lo + w))
            weights.append(jnp.where(hit, p[e:e + 1, :], 0.0).astype(BF16))
        return _dot_tn(jnp.concatenate(weights, axis=0), buf_rows)

    acc = scatter(0, ybuf[cur].reshape(N_EXPERTS * w, D_MODEL))

    most = jnp.int32(0)
    for e in range(N_EXPERTS):
        i0 = (b * N_EXPERTS + e) * LANES + k
        span = start_ref[i0 + 1] - _floor8(start_ref[i0])
        most = jnp.maximum(most, jnp.right_shift(span + w - 1, SLOT_SHIFT))

    def extra(c, acc):
        cps = []
        for e in range(N_EXPERTS):
            _, src = window(b, k, e, c)
            cps.append(pltpu.make_async_copy(y_hbm.at[b, e, pl.ds(src, w), :], ybuf.at[cur, e], sem.at[cur, e]))
        for cp in cps:
            cp.start()
        for cp in cps:
            cp.wait()
        return acc + scatter(c, ybuf[cur].reshape(N_EXPERTS * w, D_MODEL))

    acc = lax.fori_loop(1, most, extra, acc)
    g2 = mod_ref[pl.ds(b, 1), 5 * D_MODEL:6 * D_MODEL]
    o_ref[0] = _rms(nh_ref[0] + g2 * acc, fg_ref[...])


def _combine(starts_flat, new_hx, code, probs_t, mod, final_g, y, cap):
    b, l, d = new_hx.shape
    tm = TOKEN_TILE
    n_tiles = l // tm
    grid_spec = pltpu.PrefetchScalarGridSpec(
        num_scalar_prefetch=1,
        grid=(b, n_tiles),
        in_specs=[pl.BlockSpec((1, tm, d), lambda i, j, s: (i, j, 0)),
                  pl.BlockSpec((1, N_EXPERTS, tm), lambda i, j, s: (i, 0, j)),
                  pl.BlockSpec((1, N_EXPERTS, tm), lambda i, j, s: (i, 0, j)),
                  pl.BlockSpec(mod.shape, lambda i, j, s: (0, 0)),
                  pl.BlockSpec((1, d), lambda i, j, s: (0, 0)),
                  pl.BlockSpec(memory_space=pl.ANY)],
        out_specs=pl.BlockSpec((1, tm, d), lambda i, j, s: (i, j, 0)),
        scratch_shapes=[pltpu.VMEM((COMBINE_AHEAD + 1, N_EXPERTS, SLOT_WINDOW, d), BF16),
                        pltpu.SemaphoreType.DMA((COMBINE_AHEAD + 1, N_EXPERTS))],
    )
    return pl.pallas_call(
        functools.partial(_combine_kernel, cap=cap, n_tiles=n_tiles, n_steps=b * n_tiles),
        grid_spec=grid_spec,
        out_shape=jax.ShapeDtypeStruct((b, l, d), F32),
        compiler_params=_params(("arbitrary", "arbitrary")),
        name="combine",
    )(starts_flat, new_hx, code, probs_t, mod, final_g, y)


def _layer(x, c, ctx, c_ctx, lb_logits, w_mod, b_mod, norm1_g, w_in, hgrn_norm_g, w_a, w_pool, pool_scale,
           w_b, w_out, norm2_g, w_router, w_e_gate, w_e_up, w_e_down, final_g):
    b, l, d = x.shape
    cap = CAPACITY_FACTOR * l // N_EXPERTS
    assert b < MOD_ROWS and l % (HGRN_CHUNK * HGRN_STEP_CHUNKS) == 0 and l % PROJ_TILE == 0 and l % TOKEN_TILE == 0
    assert cap % SUBLANES == 0 and cap >= SLOT_WINDOW and l % (POOL_UNROLL * POOL_TILE) == 0
    cc = jnp.zeros((MOD_ROWS, d), F32).at[:b].set(c).at[b].set(c_ctx)
    assert lb_logits.shape == (2, 2, HGRN_WIDTH)
    lbl = lb_logits.reshape(4, HGRN_WIDTH)
    row = lambda a: a.reshape(1, -1)
    w_in_bf = w_in.astype(BF16)

    mod = _adaln(cc, w_mod, b_mod)
    s0f, s0b = _ctx_states(ctx, row(norm1_g), mod, lbl, w_in_bf)
    q, v, kf, gf, kb, gb, og, p, ga, gbm, safe = _in_proj(x, row(norm1_g), mod, lbl, w_in_bf)
    per_tile = TOKEN_TILE // HGRN_CHUNK
    safe = safe[:, :, :2 * per_tile, 0].reshape(b, -1, 2, per_tile)
    flags = safe.transpose(0, 1, 3, 2).astype(I32).reshape(-1)
    o_f, o_b = _hgrn(flags, q, v, kf, gf, kb, gb, s0f, s0b)
    pooled = _pool(p, w_pool.astype(BF16), pool_scale)
    new_hx, vx, probs_t = _merge(x, o_f, o_b, og, pooled, ga, gbm, mod, row(hgrn_norm_g), row(norm2_g),
                                 w_a.astype(BF16), w_b.astype(BF16), w_out.astype(BF16),
                                 w_router.T.astype(BF16))
    code, starts = _route(probs_t, cap)
    starts_flat = starts.reshape(-1)
    xg = _dispatch(starts_flat, vx, code, cap)
    y = _moe(xg, w_e_gate, w_e_up, w_e_down, cap)
    return _combine(starts_flat, new_hx, code, probs_t, mod, row(final_g), y, cap)


def kernel(x, c, ctx, c_ctx, lb_logits, w_mod, b_mod, norm1_g, w_in, hgrn_norm_g, w_a, w_pool, pool_scale,
           w_b, w_out, norm2_g, w_router, w_e_gate, w_e_up, w_e_down, final_g):
    assert w_mod.shape[0] == 1, "single-layer trunk: the context stream only seeds the latent recurrence"
    return _layer(x, c, ctx, c_ctx, lb_logits, w_mod[0], b_mod[0], norm1_g[0], w_in[0], hgrn_norm_g[0], w_a[0],
                  w_pool[0], pool_scale[0], w_b[0], w_out[0], norm2_g[0], w_router[0], w_e_gate[0],
                  w_e_up[0], w_e_down[0], final_g)
```

```python
import functools

import numpy as np
import jax
import jax.numpy as jnp
from jax import lax
from jax.experimental import pallas as pl
from jax.experimental.pallas import tpu as pltpu

F32 = jnp.float32
BF16 = jnp.bfloat16
I32 = jnp.int32

D_MODEL = 1024
N_HEADS = 8
HEAD_DIM = 128
HGRN_WIDTH = N_HEADS * HEAD_DIM
GRID_W = 64
POOL_WINDOWS = (2, 4, 8, 16)
POOL_GROUP = 128
POOL_WIDTH = POOL_GROUP * len(POOL_WINDOWS)
N_EXPERTS = 16
CAPACITY_FACTOR = 2
EXPERT_FF = 1024
EPS = 1e-6

Q_OFF = 0
I_OFF = Q_OFF + HGRN_WIDTH
FF_OFF = I_OFF + HGRN_WIDTH
FB_OFF = FF_OFF + HGRN_WIDTH
OG_OFF = FB_OFF + HGRN_WIDTH
P_OFF = OG_OFF + HGRN_WIDTH
GA_OFF = P_OFF + POOL_WIDTH
GB_OFF = GA_OFF + D_MODEL
IN_COLS = GB_OFF + D_MODEL

LANES = 128
SUBLANES = 8
SUBLANE_SHIFT = 3
MOD_ROWS = 16
ADALN_TILE = 768
PROJ_TILE = 512
MERGE_ROWS = 512
MERGE_AHEAD = 2
TOKEN_TILE = 256
HGRN_CHUNK = 128
HGRN_BLOCK = 32
HGRN_STEP_CHUNKS = 4
HGRN_SAFE_DECAY = 80.0
POOL_HALO = 16
POOL_TILE = 256
POOL_UNROLL = 8
SLOT_SHIFT = 6
SLOT_WINDOW = 1 << SLOT_SHIFT
FFN_ROWS = 256
COMBINE_AHEAD = 2
VMEM_LIMIT = 56 * 1024 * 1024


def _dot(a, b):
    return jnp.dot(a, b, preferred_element_type=F32)


def _dot_nt(a, b):
    return lax.dot_general(a, b, (((1,), (1,)), ((), ())), preferred_element_type=F32)


def _dot_tn(a, b):
    return lax.dot_general(a, b, (((0,), (0,)), ((), ())), preferred_element_type=F32)


def _rms(x, g):
    ms = jnp.mean(x * x, axis=-1, keepdims=True)
    return x * lax.rsqrt(ms + EPS) * g


def _silu(z):
    return z * jax.nn.sigmoid(z)


def _lower_bound(a0, a1):
    m = jnp.maximum(a0, a1)
    e0 = jnp.exp(a0 - m)
    e1 = jnp.exp(a1 - m)
    return e0 / (e0 + e1)


def _split3(g):
    hi = g.astype(BF16)
    r = g - hi.astype(F32)
    mid = r.astype(BF16)
    lo = (r - mid.astype(F32)).astype(BF16)
    return hi, mid, lo


def _apply01(u01, g):
    hi, mid, lo = _split3(g)
    return _dot(u01, hi) + _dot(u01, mid) + _dot(u01, lo)


def _one_zero(mask):
    return jnp.where(mask, 1.0, 0.0).astype(BF16)


def _params(sem):
    return pltpu.CompilerParams(dimension_semantics=sem, vmem_limit_bytes=VMEM_LIMIT)


def _resident(shape, index_map):
    return pl.BlockSpec(shape, index_map, pipeline_mode=pl.Buffered(1))


def _adaln_kernel(c_ref, w_ref, b_ref, o_ref):
    c = c_ref[...]
    o_ref[...] = _dot(_silu(c).astype(BF16), w_ref[...].astype(BF16)) + b_ref[...]


def _adaln(cc, w_mod, b_mod):
    n = w_mod.shape[1]
    tn = ADALN_TILE
    assert n % tn == 0
    return pl.pallas_call(
        _adaln_kernel,
        grid=(n // tn,),
        in_specs=[pl.BlockSpec((MOD_ROWS, D_MODEL), lambda j: (0, 0)),
                  pl.BlockSpec((D_MODEL, tn), lambda j: (0, j)),
                  pl.BlockSpec((1, tn), lambda j: (0, j))],
        out_specs=pl.BlockSpec((MOD_ROWS, tn), lambda j: (0, j)),
        out_shape=jax.ShapeDtypeStruct((MOD_ROWS, n), F32),
        compiler_params=_params(("arbitrary",)),
        name="adaln",
    )(cc, w_mod, b_mod.reshape(1, n))


def _ctx_kernel(ctx_ref, g1_ref, mod_ref, lbl_ref, wi_ref, wf_ref, wb_ref, sf_ref, sb_ref, *, ctx_row):
    x = ctx_ref[0]
    n = x.shape[0]
    sh = mod_ref[ctx_row:ctx_row + 1, 0:D_MODEL]
    sc = mod_ref[ctx_row:ctx_row + 1, D_MODEL:2 * D_MODEL]
    u = (_rms(x, g1_ref[...]) * (1.0 + sc) + sh).astype(BF16)
    v = _dot(u, wi_ref[...]).astype(BF16)
    zf = _dot(u, wf_ref[...])
    zb = _dot(u, wb_ref[...])
    lbf = _lower_bound(lbl_ref[0:1, :], lbl_ref[1:2, :])
    lbb = _lower_bound(lbl_ref[2:3, :], lbl_ref[3:4, :])
    ff = lbf + (1.0 - lbf) * jax.nn.sigmoid(zf)
    fb = lbb + (1.0 - lbb) * jax.nn.sigmoid(zb)
    r = lax.broadcasted_iota(I32, (n, n), 0)
    c = lax.broadcasted_iota(I32, (n, n), 1)
    ef = _apply01(_one_zero(c > r), jnp.log(ff))
    eb = _apply01(_one_zero(c < r), jnp.log(fb))
    kf = ((1.0 - ff) * jnp.exp(ef)).astype(BF16)
    kb = ((1.0 - fb) * jnp.exp(eb)).astype(BF16)
    for h in range(N_HEADS):
        hs = slice(h * HEAD_DIM, (h + 1) * HEAD_DIM)
        sf_ref[0, h] = _dot_tn(kf[:, hs], v[:, hs])
        sb_ref[0, h] = _dot_tn(kb[:, hs], v[:, hs])


def _ctx_states(ctx, norm1_g, mod, lbl, w_in_bf):
    b, lc, d = ctx.shape
    col = lambda k: pl.BlockSpec((d, HGRN_WIDTH), lambda i, k=k: (0, k))
    state = jax.ShapeDtypeStruct((b, N_HEADS, HEAD_DIM, HEAD_DIM), F32)
    sspec = pl.BlockSpec((1, N_HEADS, HEAD_DIM, HEAD_DIM), lambda i: (i, 0, 0, 0))
    return pl.pallas_call(
        functools.partial(_ctx_kernel, ctx_row=b),
        grid=(b,),
        in_specs=[pl.BlockSpec((1, lc, d), lambda i: (i, 0, 0)),
                  pl.BlockSpec((1, d), lambda i: (0, 0)),
                  pl.BlockSpec(mod.shape, lambda i: (0, 0)),
                  pl.BlockSpec(lbl.shape, lambda i: (0, 0)),
                  col(I_OFF // HGRN_WIDTH), col(FF_OFF // HGRN_WIDTH), col(FB_OFF // HGRN_WIDTH)],
        out_specs=[sspec, sspec],
        out_shape=[state, state],
        compiler_params=_params(("arbitrary",)),
        name="ctx_state",
    )(ctx, norm1_g, mod, lbl, w_in_bf, w_in_bf, w_in_bf)


def _chunks_safe(logf):
    out = []
    for c0 in range(0, logf.shape[0], HGRN_CHUNK):
        worst = None
        for lo in range(c0, c0 + HGRN_CHUNK, HGRN_BLOCK):
            dec = jnp.sum(logf[lo:lo + HGRN_BLOCK], axis=0, keepdims=True)
            worst = dec if worst is None else jnp.minimum(worst, dec)
        ok = jnp.min(worst, axis=1, keepdims=True) >= -HGRN_SAFE_DECAY
        out.append(jnp.broadcast_to(jnp.where(ok, 1.0, 0.0), (1, LANES)))
    return out


def _inproj_kernel(x_ref, g1_ref, mod_ref, lbl_ref, w_ref,
                   q_ref, v_ref, kf_ref, gf_ref, kb_ref, gb_ref, og_ref, p_ref, ga_ref, gbm_ref, safe_ref):
    b = pl.program_id(0)
    x = x_ref[0]
    m = mod_ref[pl.ds(b, 1), :]
    sh = m[:, 0:D_MODEL]
    sc = m[:, D_MODEL:2 * D_MODEL]
    u = (_rms(x, g1_ref[...]) * (1.0 + sc) + sh).astype(BF16)

    z = _dot(u, w_ref[:, Q_OFF:I_OFF])
    q_ref[0] = _silu(z).astype(BF16)

    lbf = _lower_bound(lbl_ref[0:1, :], lbl_ref[1:2, :])
    f = lbf + (1.0 - lbf) * jax.nn.sigmoid(_dot(u, w_ref[:, FF_OFF:FB_OFF]))
    kf_ref[0] = (1.0 - f).astype(BF16)
    logf = jnp.log(f)
    gf_ref[0] = logf
    flags = _chunks_safe(logf)
    lbb = _lower_bound(lbl_ref[2:3, :], lbl_ref[3:4, :])
    f = lbb + (1.0 - lbb) * jax.nn.sigmoid(_dot(u, w_ref[:, FB_OFF:OG_OFF]))
    kb_ref[0] = (1.0 - f).astype(BF16)
    logf = jnp.log(f)
    gb_ref[0] = logf
    flags = flags + _chunks_safe(logf)
    safe_ref[0, 0] = jnp.concatenate(flags + [jnp.zeros((SUBLANES - len(flags), LANES), F32)], axis=0)

    z = _dot(u, w_ref[:, OG_OFF:P_OFF])
    og_ref[0] = _silu(z).astype(BF16)
    ga_ref[0] = jax.nn.sigmoid(_dot(u, w_ref[:, GA_OFF:GB_OFF])).astype(BF16)
    gbm_ref[0] = jax.nn.sigmoid(_dot(u, w_ref[:, GB_OFF:IN_COLS])).astype(BF16)
    p_ref[0] = _dot(u, w_ref[:, P_OFF:GA_OFF])
    v_ref[0] = _dot(u, w_ref[:, I_OFF:FF_OFF]).astype(BF16)


def _in_proj(x, norm1_g, mod, lbl, w_in_bf):
    b, l, d = x.shape
    tm = TOKEN_TILE
    tok = lambda w: pl.BlockSpec((1, tm, w), lambda i, j: (i, j, 0))
    wide = lambda dt, w=HGRN_WIDTH: jax.ShapeDtypeStruct((b, l, w), dt)
    return pl.pallas_call(
        _inproj_kernel,
        grid=(b, l // tm),
        in_specs=[tok(d),
                  pl.BlockSpec((1, d), lambda i, j: (0, 0)),
                  pl.BlockSpec(mod.shape, lambda i, j: (0, 0)),
                  pl.BlockSpec(lbl.shape, lambda i, j: (0, 0)),
                  _resident((d, IN_COLS), lambda i, j: (0, 0))],
        out_specs=[tok(HGRN_WIDTH)] * 7 + [tok(POOL_WIDTH), tok(d), tok(d),
                                            pl.BlockSpec((1, 1, SUBLANES, LANES), lambda i, j: (i, j, 0, 0))],
        out_shape=[wide(BF16), wide(BF16), wide(BF16), wide(F32), wide(BF16), wide(F32), wide(BF16),
                   wide(F32, POOL_WIDTH), wide(BF16, d), wide(BF16, d),
                   jax.ShapeDtypeStruct((b, l // tm, SUBLANES, LANES), F32)],
        compiler_params=_params(("arbitrary", "arbitrary")),
        name="in_proj",
    )(x, norm1_g, mod, lbl, w_in_bf)


def _reads(i, j, reverse):
    return i <= j if reverse else i >= j


def _block_edges(b_ref, reverse):
    nb = HGRN_CHUNK // HGRN_BLOCK
    zero = jnp.zeros((1, HGRN_WIDTH), F32)
    out = []
    for j in range(nb):
        if reverse:
            r = (j + 1) * HGRN_BLOCK
            out.append(zero if j == nb - 1 else b_ref[r:r + 1, :])
        else:
            r = j * HGRN_BLOCK
            out.append(zero if j == 0 else b_ref[r - 1:r, :])
    return out


def _seen(reverse):
    t = HGRN_CHUNK
    row = lax.broadcasted_iota(I32, (t, t), 0)
    col = lax.broadcasted_iota(I32, (t, t), 1)
    return (col >= row) if reverse else (col <= row)


def _hgrn_decay(g_ref, b_ref, reverse):
    g = g_ref[...]
    hi = g.astype(BF16)
    lo = (g - hi.astype(F32)).astype(BF16)
    u01 = _one_zero(_seen(reverse))
    b_ref[...] = _dot(u01, hi) + _dot(u01, lo)


def _chunk_end(b_ref, reverse):
    return b_ref[0:1, :] if reverse else b_ref[HGRN_CHUNK - 1:HGRN_CHUNK, :]


def _decay_columns(b_end):
    e_end = jnp.exp(b_end)
    rows = [e_end[:, h * HEAD_DIM:(h + 1) * HEAD_DIM] for h in range(N_HEADS)]
    return jnp.concatenate(rows + [jnp.zeros((HEAD_DIM - N_HEADS, HEAD_DIM), F32)], axis=0).T


def _hgrn_fast(q_ref, v_ref, k_ref, b_ref, o_ref, s_ref, reverse):
    t, bl = HGRN_CHUNK, HGRN_BLOCK
    nb = t // bl
    seen = _seen(reverse)
    edges = _block_edges(b_ref, reverse)
    gain = {(i, j): jnp.exp(edges[i] - edges[j])
            for i in range(nb) for j in range(nb) if i != j and _reads(i, j, reverse)}
    b_end = _chunk_end(b_ref, reverse)
    e_cols = _decay_columns(b_end)
    zblk = jnp.zeros((bl, HEAD_DIM), BF16)
    full = nb - 1 if reverse else 0
    heads = [slice(h * HEAD_DIM, (h + 1) * HEAD_DIM) for h in range(N_HEADS)]

    scores, q_full, k_state = [], [], []
    for hs in heads:
        binc = b_ref[:, hs]
        q = q_ref[:, hs].astype(F32)
        k = k_ref[:, hs].astype(F32)
        q_own, k_own = [], []
        for i in range(nb):
            bs = slice(i * bl, (i + 1) * bl)
            q_own.append(q[bs] * jnp.exp(binc[bs] - edges[i][:, hs]))
            k_own.append((k[bs] * jnp.exp(edges[i][:, hs] - binc[bs])).astype(BF16))
        q_slabs = []
        for j in range(nb):
            parts = []
            for i in range(nb):
                if not _reads(i, j, reverse):
                    parts.append(zblk)
                elif i == j:
                    parts.append(q_own[i].astype(BF16))
                else:
                    parts.append((q_own[i] * gain[i, j][:, hs]).astype(BF16))
            q_slabs.append(jnp.concatenate(parts, axis=0))
        k_slabs = [jnp.concatenate([k_own[i] if i == j else zblk for i in range(nb)], axis=0)
                   for j in range(nb)]
        near = slice(nb // 2, nb) if reverse else slice(0, nb // 2)
        near_rows = slice(t // 2, t) if reverse else slice(0, t // 2)
        far_rows = slice(0, t // 2) if reverse else slice(t // 2, t)
        s_near = _dot_nt(jnp.concatenate([s[near_rows] for s in q_slabs[near]], axis=1),
                         jnp.concatenate(k_slabs[near], axis=1))
        s_far = _dot_nt(jnp.concatenate([s[far_rows] for s in q_slabs], axis=1),
                        jnp.concatenate(k_slabs, axis=1))
        scores.append(jnp.concatenate([s_far, s_near] if reverse else [s_near, s_far], axis=0))
        q_full.append(q_slabs[full])
        k_state.append((k * jnp.exp(b_end[:, hs] - binc)).astype(BF16))

    def readout():
        for h, hs in enumerate(heads):
            p = jnp.where(seen, scores[h], 0.0).astype(BF16)
            lhs = jnp.concatenate([p, q_full[h]], axis=1)
            rhs = jnp.concatenate([v_ref[:, hs], s_ref[h].astype(BF16)], axis=0)
            o_ref[:, hs] = _dot(lhs, rhs).astype(o_ref.dtype)

    def update():
        for h, hs in enumerate(heads):
            s_ref[h] = s_ref[h] * e_cols[:, h:h + 1] + _dot_tn(k_state[h], v_ref[:, hs])

    return readout, update


def _hgrn_slow(q_ref, v_ref, k_ref, b_ref, o_ref, s_ref, oi_ref, kk_ref, vv_ref, reverse):
    t = HGRN_CHUNK
    q = q_ref[...].astype(F32)
    binc = b_ref[...]
    kk_ref[...] = k_ref[...].astype(F32)
    vv_ref[...] = v_ref[...].astype(F32)
    oi_ref[...] = jnp.zeros((t, HGRN_WIDTH), F32)
    trow = lax.broadcasted_iota(I32, (t, HGRN_WIDTH), 0)
    heads = [slice(h * HEAD_DIM, (h + 1) * HEAD_DIM) for h in range(N_HEADS)]

    def body(s, carry):
        reads = (trow <= s) if reverse else (trow >= s)
        w = jnp.exp(jnp.where(reads, binc - b_ref[pl.ds(s, 1), :], 0.0))
        prod = jnp.where(reads, q * kk_ref[pl.ds(s, 1), :] * w, 0.0)
        vs = vv_ref[pl.ds(s, 1), :]
        for hs in heads:
            oi_ref[:, hs] += jnp.sum(prod[:, hs], axis=1, keepdims=True) * vs[:, hs]
        return carry

    lax.fori_loop(0, t, body, 0)
    b_end = _chunk_end(b_ref, reverse)
    e_cols = _decay_columns(b_end)
    q_full = (q * jnp.exp(binc)).astype(BF16)
    k_state = (kk_ref[...] * jnp.exp(b_end - binc)).astype(BF16)
    for h, hs in enumerate(heads):
        o_ref[:, hs] = (oi_ref[:, hs] + _dot(q_full[:, hs], s_ref[h].astype(BF16))).astype(o_ref.dtype)
        s_ref[h] = s_ref[h] * e_cols[:, h:h + 1] + _dot_tn(k_state[:, hs], v_ref[:, hs])


def _hgrn_kernel(flag_ref, qf_ref, vf_ref, kf_ref, gf_ref, qb_ref, vb_ref, kb_ref, gb_ref, s0f_ref, s0b_ref,
                 of_ref, ob_ref, sf_ref, sb_ref, bf_ref, bb_ref, oi_ref, kk_ref, vv_ref):
    b = pl.program_id(0)
    j = pl.program_id(1)
    n = pl.num_programs(1) * HGRN_STEP_CHUNKS
    t = HGRN_CHUNK

    @pl.when(j == 0)
    def _():
        sf_ref[...] = s0f_ref[0]
        sb_ref[...] = s0b_ref[0]

    def chunk(s, carry):
        rf = pl.ds(pl.multiple_of(s * t, t), t)
        rb = pl.ds(pl.multiple_of((HGRN_STEP_CHUNKS - 1 - s) * t, t), t)
        qf, vf, kf, gf, of = (r.at[0, rf, :] for r in (qf_ref, vf_ref, kf_ref, gf_ref, of_ref))
        qb, vb, kb, gb, ob = (r.at[0, rb, :] for r in (qb_ref, vb_ref, kb_ref, gb_ref, ob_ref))
        c = j * HGRN_STEP_CHUNKS + s
        safe = (flag_ref[(b * n + c) * 2] + flag_ref[(b * n + n - 1 - c) * 2 + 1]) == 2

        @pl.when(safe)
        def _():
            _hgrn_decay(gf, bf_ref, False)
            _hgrn_decay(gb, bb_ref, True)
            read_f, update_f = _hgrn_fast(qf, vf, kf, bf_ref, of, sf_ref, False)
            read_b, update_b = _hgrn_fast(qb, vb, kb, bb_ref, ob, sb_ref, True)
            read_f()
            read_b()
            update_f()
            update_b()

        @pl.when(jnp.logical_not(safe))
        def _():
            _hgrn_decay(gf, bf_ref, False)
            _hgrn_decay(gb, bb_ref, True)
            _hgrn_slow(qf, vf, kf, bf_ref, of, sf_ref, oi_ref, kk_ref, vv_ref, False)
            _hgrn_slow(qb, vb, kb, bb_ref, ob, sb_ref, oi_ref, kk_ref, vv_ref, True)

        return carry

    lax.fori_loop(0, HGRN_STEP_CHUNKS, chunk, 0)


def _hgrn(flags, q, v, kf, gf, kb, gb, s0f, s0b):
    b, l, w = q.shape
    t = HGRN_CHUNK
    rows_per_step = t * HGRN_STEP_CHUNKS
    n = l // rows_per_step
    fwd = pl.BlockSpec((1, rows_per_step, w), lambda i, j, f: (i, j, 0))
    bwd = pl.BlockSpec((1, rows_per_step, w), lambda i, j, f: (i, n - 1 - j, 0))
    sspec = pl.BlockSpec((1, N_HEADS, HEAD_DIM, HEAD_DIM), lambda i, j, f: (i, 0, 0, 0))
    out = jax.ShapeDtypeStruct((b, l, w), BF16)
    state = pltpu.VMEM((N_HEADS, HEAD_DIM, HEAD_DIM), F32)
    rows = pltpu.VMEM((t, w), F32)
    grid_spec = pltpu.PrefetchScalarGridSpec(
        num_scalar_prefetch=1,
        grid=(b, n),
        in_specs=[fwd, fwd, fwd, fwd, bwd, bwd, bwd, bwd, sspec, sspec],
        out_specs=[fwd, bwd],
        scratch_shapes=[state, state, rows, rows, rows, rows, rows],
    )
    return pl.pallas_call(
        _hgrn_kernel,
        grid_spec=grid_spec,
        out_shape=[out, out],
        compiler_params=_params(("arbitrary", "arbitrary")),
        name="hgrn",
    )(flags, q, v, kf, gf, q, v, kb, gb, s0f, s0b)


def _pool_col_matrices():
    mats = np.zeros((len(POOL_WINDOWS), POOL_TILE, POOL_TILE), np.float32)
    for j, w in enumerate(POOL_WINDOWS):
        for t in range(POOL_TILE):
            r, c = divmod(t, GRID_W)
            lo, hi = max(c - w // 2, 0), min(c + w // 2 - 1, GRID_W - 1)
            mats[j, t, r * GRID_W + lo:r * GRID_W + hi + 1] = 1.0
    return jnp.asarray(mats, BF16)


def _window_len(pos, half, n):
    return jnp.minimum(pos + half - 1, n - 1) + 1 - jnp.maximum(pos - half, 0)


def _pool_kernel(p_ref, a_ref, wp_ref, ps_ref, o_ref, pad_ref, inr_ref, inc_ref, *, rows):
    j = pl.program_id(1)
    half = jnp.left_shift(1, j)
    gw = GRID_W
    halo = POOL_HALO * gw
    l = rows * gw
    tile_rows = POOL_TILE // gw
    edge = POOL_HALO // 2
    for buf in range(2):
        pad_ref[buf, 0:halo, :] = jnp.zeros((halo, POOL_GROUP), F32)
        pad_ref[buf, halo + l:2 * halo + l, :] = jnp.zeros((halo, POOL_GROUP), F32)
    pad_ref[0, halo:halo + l, :] = p_ref[0]

    def level(src, dst, back, fwd):
        def body(g, carry):
            for u in range(8):
                t = pl.multiple_of((edge + g * 8 + u) * gw, gw)
                pad_ref[dst, pl.ds(t, gw), :] = (pad_ref[src, pl.ds(t - back * gw, gw), :]
                                                 + pad_ref[src, pl.ds(t + fwd * gw, gw), :])
            return carry

        lax.fori_loop(0, (rows + 2 * (POOL_HALO - edge)) // 8, body, 0)

    level(0, 1, 1, 0)

    @pl.when(j >= 1)
    def _():
        level(1, 0, 1, 1)

    @pl.when(j >= 2)
    def _():
        level(0, 1, 2, 2)

    @pl.when(j >= 3)
    def _():
        level(1, 0, 4, 4)

    summed = 1 - jnp.bitwise_and(j, 1)
    r = lax.broadcasted_iota(I32, inr_ref.shape, 0)
    inr_ref[...] = 1.0 / _window_len(r, half, rows).astype(F32)
    c = jnp.bitwise_and(lax.broadcasted_iota(I32, inc_ref.shape, 0), gw - 1)
    inc_ref[...] = 1.0 / _window_len(c, half, gw).astype(F32)
    a = a_ref[0]
    wp = wp_ref[0]
    scale = ps_ref[0]

    def step(i, carry):
        tiles = [POOL_UNROLL * i + u for u in range(POOL_UNROLL)]
        starts = [pl.multiple_of(t * POOL_TILE, POOL_TILE) for t in tiles]
        sums = []
        for t0 in starts:
            rsum = pad_ref[summed, pl.ds(halo + t0, POOL_TILE), :]
            hi = rsum.astype(BF16)
            lo = (rsum - hi.astype(F32)).astype(BF16)
            sums.append(_dot(a, jnp.concatenate([hi, lo], axis=1)))
        resid = []
        for t, t0, both in zip(tiles, starts, sums):
            total = both[:, :POOL_GROUP] + both[:, POOL_GROUP:]
            by_row = [total[rr * gw:(rr + 1) * gw] * inr_ref[pl.ds(t * tile_rows + rr, 1), :]
                      for rr in range(tile_rows)]
            resid.append((jnp.concatenate(by_row, axis=0) * inc_ref[...]
                          - p_ref[0, pl.ds(t0, POOL_TILE), :]).astype(BF16))
        for t0, rb in zip(starts, resid):
            o_ref[0, pl.ds(t0, POOL_TILE), :] = (_dot(rb, wp) * scale).astype(BF16)
        return carry

    lax.fori_loop(0, l // (POOL_UNROLL * POOL_TILE), step, 0)


def _pool(p, w_pool_bf, pool_scale):
    b, l, _ = p.shape
    rows = l // GRID_W
    ng = len(POOL_WINDOWS)
    grp = pl.BlockSpec((1, l, POOL_GROUP), lambda i, j: (i, 0, j))
    return pl.pallas_call(
        functools.partial(_pool_kernel, rows=rows),
        grid=(b, ng),
        in_specs=[grp,
                  pl.BlockSpec((1, POOL_TILE, POOL_TILE), lambda i, j: (j, 0, 0)),
                  pl.BlockSpec((1, POOL_GROUP, POOL_GROUP), lambda i, j: (j, 0, 0)),
                  pl.BlockSpec((1, 1, POOL_GROUP), lambda i, j: (j, 0, 0))],
        out_specs=grp,
        out_shape=jax.ShapeDtypeStruct((b, l, POOL_WIDTH), BF16),
        scratch_shapes=[pltpu.VMEM((2, (rows + 2 * POOL_HALO) * GRID_W, POOL_GROUP), F32),
                        pltpu.VMEM((rows, POOL_GROUP), F32),
                        pltpu.VMEM((POOL_TILE, POOL_GROUP), F32)],
        compiler_params=_params(("arbitrary", "arbitrary")),
        name="pool",
    )(p, _pool_col_matrices(), w_pool_bf, pool_scale.reshape(ng, 1, POOL_GROUP))


def _merge_kernel(x_hbm, of_hbm, ob_hbm, og_hbm, pl_hbm, ga_hbm, gb_hbm, mod_ref, hg_ref, n2_ref,
                  wa_ref, wb_ref, wo_ref, wr_ref, nh_ref, vx_ref, pr_ref,
                  x_ring, of_ring, ob_ring, og_ring, pl_ring, ga_ring, gb_ring, sem):
    b = pl.program_id(0)
    j = pl.program_id(1)
    nj = pl.num_programs(1)
    step = b * nj + j
    n_steps = pl.num_programs(0) * nj
    tm = x_ring.shape[1]
    streams = ((x_hbm, x_ring), (of_hbm, of_ring), (ob_hbm, ob_ring), (og_hbm, og_ring),
               (pl_hbm, pl_ring), (ga_hbm, ga_ring), (gb_hbm, gb_ring))

    def copies(ahead, slot):
        jj = j + ahead
        wrap = jj >= nj
        bb = jnp.where(wrap, b + 1, b)
        rows = pl.ds(pl.multiple_of(jnp.where(wrap, jj - nj, jj) * tm, tm), tm)
        return [pltpu.make_async_copy(src.at[bb, rows, :], ring.at[slot], sem.at[slot, n])
                for n, (src, ring) in enumerate(streams)]

    @pl.when(step == 0)
    def _():
        for a in range(MERGE_AHEAD):
            for cp in copies(a, a):
                cp.start()

    cur = lax.rem(step, MERGE_AHEAD + 1)

    @pl.when(step + MERGE_AHEAD < n_steps)
    def _():
        for cp in copies(MERGE_AHEAD, lax.rem(step + MERGE_AHEAD, MERGE_AHEAD + 1)):
            cp.start()

    for cp in copies(0, cur):
        cp.wait()

    m = mod_ref[pl.ds(b, 1), :]
    g1 = m[:, 2 * D_MODEL:3 * D_MODEL]
    sh2 = m[:, 3 * D_MODEL:4 * D_MODEL]
    sc2 = m[:, 4 * D_MODEL:5 * D_MODEL]
    hg = hg_ref[...]
    for r0 in range(0, tm, MERGE_ROWS):
        rs = slice(r0, r0 + MERGE_ROWS)
        gated = []
        for h in range(N_HEADS):
            hs = slice(h * HEAD_DIM, (h + 1) * HEAD_DIM)
            o = of_ring[cur, rs, hs].astype(F32) + ob_ring[cur, rs, hs].astype(F32)
            gated.append((_rms(o, hg) * og_ring[cur, rs, hs].astype(F32)).astype(BF16))
        ya = _dot(jnp.concatenate(gated, axis=1), wa_ref[...])
        yb = _dot(pl_ring[cur, rs, :], wb_ref[...])
        y = ga_ring[cur, rs, :].astype(F32) * ya + gb_ring[cur, rs, :].astype(F32) * yb
        nh = x_ring[cur, rs, :] + g1 * _dot(y.astype(BF16), wo_ref[...])
        nh_ref[0, rs, :] = nh
        vx = (_rms(nh, n2_ref[...]) * (1.0 + sc2) + sh2).astype(BF16)
        vx_ref[0, rs, :] = vx
        logits = _dot_nt(wr_ref[...], vx)
        ex = jnp.exp(logits - jnp.max(logits, axis=0, keepdims=True))
        pr_ref[0, :, rs] = ex / jnp.sum(ex, axis=0, keepdims=True)


def _merge(x, o_f, o_b, og, pooled, ga, gb, mod, hgrn_g, norm2_g, w_a_bf, w_b_bf, w_out_bf, w_router_t):
    b, l, d = x.shape
    tm = PROJ_TILE
    tok = lambda w: pl.BlockSpec((1, tm, w), lambda i, j: (i, j, 0))
    const = lambda a: pl.BlockSpec(a.shape, lambda i, j: (0,) * a.ndim)
    hbm = pl.BlockSpec(memory_space=pl.ANY)
    ring = lambda w, dt: pltpu.VMEM((MERGE_AHEAD + 1, tm, w), dt)
    assert b * (l // tm) >= MERGE_AHEAD
    return pl.pallas_call(
        _merge_kernel,
        grid=(b, l // tm),
        in_specs=[hbm] * 7 + [
                  const(mod), const(hgrn_g), const(norm2_g),
                  _resident(w_a_bf.shape, lambda i, j: (0, 0)),
                  _resident(w_b_bf.shape, lambda i, j: (0, 0)),
                  _resident(w_out_bf.shape, lambda i, j: (0, 0)),
                  const(w_router_t)],
        out_specs=[tok(d), tok(d), pl.BlockSpec((1, N_EXPERTS, tm), lambda i, j: (i, 0, j))],
        out_shape=[jax.ShapeDtypeStruct((b, l, d), F32),
                   jax.ShapeDtypeStruct((b, l, d), BF16),
                   jax.ShapeDtypeStruct((b, N_EXPERTS, l), F32)],
        scratch_shapes=[ring(d, F32), ring(d, BF16), ring(d, BF16), ring(d, BF16), ring(POOL_WIDTH, BF16),
                        ring(d, BF16), ring(d, BF16), pltpu.SemaphoreType.DMA((MERGE_AHEAD + 1, 7))],
        compiler_params=_params(("arbitrary", "arbitrary")),
        name="merge",
    )(x, o_f, o_b, og, pooled, ga, gb, mod, hgrn_g, norm2_g, w_a_bf, w_b_bf, w_out_bf, w_router_t)


def _route_kernel(p_ref, code_ref, start_ref, *, cap):
    l = p_ref.shape[2]
    tm = TOKEN_TILE
    bits = lax.bitcast_convert_type(p_ref[0], I32)
    capf = jnp.float32(cap)

    def count_ge(cand):
        return jnp.sum(jnp.where(bits >= cand, 1.0, 0.0), axis=1, keepdims=True)

    def search(i, prefix):
        cand = jnp.bitwise_or(prefix, jnp.left_shift(1, 30 - i))
        return jnp.where(count_ge(cand) >= capf, cand, prefix)

    tau = lax.fori_loop(0, 31, search, jnp.zeros((N_EXPERTS, 1), I32))
    need = capf - jnp.sum(jnp.where(bits > tau, 1.0, 0.0), axis=1, keepdims=True)
    before = _one_zero(lax.broadcasted_iota(I32, (tm, tm), 0) < lax.broadcasted_iota(I32, (tm, tm), 1))
    lane = lax.broadcasted_iota(I32, (N_EXPERTS, LANES), 1)

    def tile(k, carry):
        n_gt, n_eq, starts = carry
        sl = pl.ds(pl.multiple_of(k * tm, tm), tm)
        bk = lax.bitcast_convert_type(p_ref[0, :, sl], I32)
        gt = bk > tau
        eq = bk == tau
        gt01 = _one_zero(gt)
        eq01 = _one_zero(eq)
        eq_before = n_eq + _dot(eq01, before)
        chosen = jnp.logical_or(gt, jnp.logical_and(eq, eq_before < need))
        rank = n_gt + _dot(gt01, before) + jnp.minimum(eq_before, need)
        code_ref[0, :, sl] = jnp.where(chosen, rank, -1.0).astype(I32)
        starts = jnp.where(lane == k, (n_gt + jnp.minimum(n_eq, need)).astype(I32), starts)
        n_gt = n_gt + jnp.sum(gt01.astype(F32), axis=1, keepdims=True)
        n_eq = n_eq + jnp.sum(eq01.astype(F32), axis=1, keepdims=True)
        return n_gt, n_eq, starts

    zero = jnp.zeros((N_EXPERTS, 1), F32)
    _, _, starts = lax.fori_loop(0, l // tm, tile, (zero, zero, jnp.zeros((N_EXPERTS, LANES), I32)))
    start_ref[0] = jnp.where(lane == l // tm, cap, starts)


def _route(probs_t, cap):
    b, e, l = probs_t.shape
    assert l // TOKEN_TILE < LANES
    return pl.pallas_call(
        functools.partial(_route_kernel, cap=cap),
        grid=(b,),
        in_specs=[pl.BlockSpec((1, e, l), lambda i: (i, 0, 0))],
        out_specs=[pl.BlockSpec((1, e, l), lambda i: (i, 0, 0)),
                   pl.BlockSpec((1, e, LANES), lambda i: (i, 0, 0))],
        out_shape=[jax.ShapeDtypeStruct((b, e, l), I32), jax.ShapeDtypeStruct((b, e, LANES), I32)],
        compiler_params=_params(("arbitrary",)),
        name="route",
    )(probs_t)


def _floor8(n):
    return jnp.left_shift(jnp.right_shift(n, SUBLANE_SHIFT), SUBLANE_SHIFT)


def _dispatch_kernel(start_ref, vx_ref, code_ref, xg_hbm, stage_ref, carry_ref, sent_ref, sem,
                     *, cap, n_tiles, n_steps):
    b = pl.program_id(0)
    k = pl.program_id(1)
    step = b * n_tiles + k
    w = SLOT_WINDOW
    tm = TOKEN_TILE
    ne = N_EXPERTS

    @pl.when(step == 0)
    def _():
        sent_ref[0] = 0

    @pl.when(k == 0)
    def _():
        carry_ref[...] = jnp.zeros(carry_ref.shape, F32)

    base, tail = [], []
    n_pass = jnp.int32(1)
    for e in range(ne):
        i0 = (b * ne + e) * LANES + k
        end = start_ref[i0 + 1]
        base.append(_floor8(start_ref[i0]))
        tail.append(_floor8(end))
        n_pass = jnp.maximum(n_pass, jnp.right_shift(end - base[e] + w - 1, SLOT_SHIFT))

    def send(buf, rows):
        @pl.when(sent_ref[0] > 0)
        def _():
            for e in range(ne):
                pltpu.make_async_copy(stage_ref.at[0, pl.ds(e * w, w), :], xg_hbm.at[b, e, pl.ds(0, w), :],
                                      sem.at[e]).wait()

        for e in range(ne):
            pltpu.make_async_copy(stage_ref.at[buf, pl.ds(e * w, w), :],
                                  xg_hbm.at[b, e, pl.ds(pl.multiple_of(rows[e], SUBLANES), w), :],
                                  sem.at[e]).start()
        sent_ref[0] = sent_ref[0] + 1

    slot = lax.broadcasted_iota(I32, (w, tm), 0)
    g8 = SUBLANES
    group = lax.broadcasted_iota(I32, (g8, tm), 0)

    def one_pass(c, carry):
        buf = jnp.bitwise_and(sent_ref[0], 1)
        hits = [_one_zero(code_ref[0, e:e + 1, :] == (slot + (base[e] + c * w))) for e in range(ne)]
        tails = [_one_zero(code_ref[0, e:e + 1, :] == (group + tail[e])) for e in range(ne)]
        rows = _dot(jnp.concatenate(hits + tails, axis=0), vx_ref[0])
        stage_ref[buf] = rows[:ne * w].astype(BF16)

        @pl.when(c == 0)
        def _():
            for e in range(ne):
                first = rows[e * w:e * w + g8] + carry_ref[e]
                stage_ref[buf, e * w:e * w + g8, :] = first.astype(BF16)
                own = rows[ne * w + g8 * e:ne * w + g8 * (e + 1)]
                carry_ref[e] = jnp.where(tail[e] == base[e], carry_ref[e], 0.0) + own

        send(buf, [jnp.minimum(base[e] + c * w, cap) for e in range(ne)])
        return carry

    lax.fori_loop(0, n_pass, one_pass, 0)

    @pl.when(k == n_tiles - 1)
    def _():
        buf = jnp.bitwise_and(sent_ref[0], 1)
        stage_ref[buf] = jnp.zeros(stage_ref.shape[1:], BF16)
        send(buf, [cap] * ne)

    @pl.when(step == n_steps - 1)
    def _():
        for e in range(ne):
            pltpu.make_async_copy(stage_ref.at[0, pl.ds(e * w, w), :], xg_hbm.at[b, e, pl.ds(0, w), :],
                                  sem.at[e]).wait()


def _dispatch(starts_flat, vx, code, cap):
    b, l, d = vx.shape
    tm = TOKEN_TILE
    n_tiles = l // tm
    grid_spec = pltpu.PrefetchScalarGridSpec(
        num_scalar_prefetch=1,
        grid=(b, n_tiles),
        in_specs=[pl.BlockSpec((1, tm, d), lambda i, j, s: (i, j, 0)),
                  pl.BlockSpec((1, N_EXPERTS, tm), lambda i, j, s: (i, 0, j))],
        out_specs=pl.BlockSpec(memory_space=pl.ANY),
        scratch_shapes=[pltpu.VMEM((2, N_EXPERTS * SLOT_WINDOW, d), BF16),
                        pltpu.VMEM((N_EXPERTS, SUBLANES, d), F32),
                        pltpu.SMEM((1,), I32),
                        pltpu.SemaphoreType.DMA((N_EXPERTS,))],
    )
    return pl.pallas_call(
        functools.partial(_dispatch_kernel, cap=cap, n_tiles=n_tiles, n_steps=b * n_tiles),
        grid_spec=grid_spec,
        out_shape=jax.ShapeDtypeStruct((b, N_EXPERTS, cap + SLOT_WINDOW, d), BF16),
        compiler_params=_params(("arbitrary", "arbitrary")),
        name="dispatch",
    )(starts_flat, vx, code)


def _moe_kernel(xg_ref, wg_ref, wu_ref, wd_ref, y_ref, wgb_ref, wub_ref, wdb_ref, *, cap):
    @pl.when(pl.program_id(1) == 0)
    def _():
        wgb_ref[...] = wg_ref[0].astype(BF16)
        wub_ref[...] = wu_ref[0].astype(BF16)
        wdb_ref[...] = wd_ref[0].astype(BF16)

    fr = min(FFN_ROWS, cap)
    for r0 in range(0, cap, fr):
        xg = xg_ref[0, 0, r0:r0 + fr, :]
        hid = (_silu(_dot(xg, wgb_ref[...])) * _dot(xg, wub_ref[...])).astype(BF16)
        y_ref[0, 0, r0:r0 + fr, :] = _dot(hid, wdb_ref[...]).astype(BF16)


def _moe(xg, w_gate, w_up, w_down, cap):
    b, ne, _, d = xg.shape
    ff = w_gate.shape[2]
    rows = pl.BlockSpec((1, 1, cap, d), lambda e, i: (i, e, 0, 0))
    return pl.pallas_call(
        functools.partial(_moe_kernel, cap=cap),
        grid=(ne, b),
        in_specs=[rows,
                  pl.BlockSpec((1, d, ff), lambda e, i: (e, 0, 0)),
                  pl.BlockSpec((1, d, ff), lambda e, i: (e, 0, 0)),
                  pl.BlockSpec((1, ff, d), lambda e, i: (e, 0, 0))],
        out_specs=rows,
        out_shape=jax.ShapeDtypeStruct((b, ne, cap, d), BF16),
        scratch_shapes=[pltpu.VMEM((d, ff), BF16), pltpu.VMEM((d, ff), BF16), pltpu.VMEM((ff, d), BF16)],
        compiler_params=_params(("arbitrary", "arbitrary")),
        name="moe",
    )(xg, w_gate, w_up, w_down)


def _combine_kernel(start_ref, nh_ref, code_ref, p_ref, mod_ref, fg_ref, y_hbm, o_ref, ybuf, sem,
                    *, cap, n_tiles, n_steps):
    b = pl.program_id(0)
    k = pl.program_id(1)
    step = b * n_tiles + k
    tm = TOKEN_TILE
    w = SLOT_WINDOW

    def window(bb, kk, e, c):
        st = start_ref[(bb * N_EXPERTS + e) * LANES + kk]
        lo = _floor8(st) + c * w
        return lo, pl.multiple_of(jnp.minimum(lo, cap - w), SUBLANES)

    def copies(bb, kk, buf):
        out = []
        for e in range(N_EXPERTS):
            _, src = window(bb, kk, e, 0)
            out.append(pltpu.make_async_copy(y_hbm.at[bb, e, pl.ds(src, w), :], ybuf.at[buf, e], sem.at[buf, e]))
        return out

    def later(ahead):
        kk = k + ahead
        wrap = kk >= n_tiles
        return jnp.where(wrap, b + 1, b), jnp.where(wrap, kk - n_tiles, kk)

    @pl.when(step == 0)
    def _():
        for a in range(COMBINE_AHEAD):
            for cp in copies(*later(a), a):
                cp.start()

    cur = lax.rem(step, COMBINE_AHEAD + 1)

    @pl.when(step + COMBINE_AHEAD < n_steps)
    def _():
        for cp in copies(*later(COMBINE_AHEAD), lax.rem(step + COMBINE_AHEAD, COMBINE_AHEAD + 1)):
            cp.start()

    for cp in copies(b, k, cur):
        cp.wait()

    slot = lax.broadcasted_iota(I32, (w, tm), 0)
    p = p_ref[0]

    def scatter(c, buf_rows):
        weights = []
        for e in range(N_EXPERTS):
            lo, src = window(b, k, e, c)
            codes = code_ref[0, e:e + 1, :]
            hit = jnp.logical_and(codes == (slot + src), jnp.logical_and(codes >= lo, codes < lo + w))
            weights.append(jnp.where(hit, p[e:e + 1, :], 0.0).astype(BF16))
        return _dot_tn(jnp.concatenate(weights, axis=0), buf_rows)

    acc = scatter(0, ybuf[cur].reshape(N_EXPERTS * w, D_MODEL))

    most = jnp.int32(0)
    for e in range(N_EXPERTS):
        i0 = (b * N_EXPERTS + e) * LANES + k
        span = start_ref[i0 + 1] - _floor8(start_ref[i0])
        most = jnp.maximum(most, jnp.right_shift(span + w - 1, SLOT_SHIFT))

    def extra(c, acc):
        cps = []
        for e in range(N_EXPERTS):
            _, src = window(b, k, e, c)
            cps.append(pltpu.make_async_copy(y_hbm.at[b, e, pl.ds(src, w), :], ybuf.at[cur, e], sem.at[cur, e]))
        for cp in cps:
            cp.start()
        for cp in cps:
            cp.wait()
        return acc + scatter(c, ybuf[cur].reshape(N_EXPERTS * w, D_MODEL))

    acc = lax.fori_loop(1, most, extra, acc)
    g2 = mod_ref[pl.ds(b, 1), 5 * D_MODEL:6 * D_MODEL]
    o_ref[0] = _rms(nh_ref[0] + g2 * acc, fg_ref[...])


def _combine(starts_flat, new_hx, code, probs_t, mod, final_g, y, cap):
    b, l, d = new_hx.shape
    tm = TOKEN_TILE
    n_tiles = l // tm
    grid_spec = pltpu.PrefetchScalarGridSpec(
        num_scalar_prefetch=1,
        grid=(b, n_tiles),
        in_specs=[pl.BlockSpec((1, tm, d), lambda i, j, s: (i, j, 0)),
                  pl.BlockSpec((1, N_EXPERTS, tm), lambda i, j, s: (i, 0, j)),
                  pl.BlockSpec((1, N_EXPERTS, tm), lambda i, j, s: (i, 0, j)),
                  pl.BlockSpec(mod.shape, lambda i, j, s: (0, 0)),
                  pl.BlockSpec((1, d), lambda i, j, s: (0, 0)),
                  pl.BlockSpec(memory_space=pl.ANY)],
        out_specs=pl.BlockSpec((1, tm, d), lambda i, j, s: (i, j, 0)),
        scratch_shapes=[pltpu.VMEM((COMBINE_AHEAD + 1, N_EXPERTS, SLOT_WINDOW, d), BF16),
                        pltpu.SemaphoreType.DMA((COMBINE_AHEAD + 1, N_EXPERTS))],
    )
    return pl.pallas_call(
        functools.partial(_combine_kernel, cap=cap, n_tiles=n_tiles, n_steps=b * n_tiles),
        grid_spec=grid_spec,
        out_shape=jax.ShapeDtypeStruct((b, l, d), F32),
        compiler_params=_params(("arbitrary", "arbitrary")),
        name="combine",
    )(starts_flat, new_hx, code, probs_t, mod, final_g, y)


def _layer(x, c, ctx, c_ctx, lb_logits, w_mod, b_mod, norm1_g, w_in, hgrn_norm_g, w_a, w_pool, pool_scale,
           w_b, w_out, norm2_g, w_router, w_e_gate, w_e_up, w_e_down, final_g):
    b, l, d = x.shape
    cap = CAPACITY_FACTOR * l // N_EXPERTS
    assert b < MOD_ROWS and l % (HGRN_CHUNK * HGRN_STEP_CHUNKS) == 0 and l % PROJ_TILE == 0 and l % TOKEN_TILE == 0
    assert cap % SUBLANES == 0 and cap >= SLOT_WINDOW and l % (POOL_UNROLL * POOL_TILE) == 0
    cc = jnp.zeros((MOD_ROWS, d), F32).at[:b].set(c).at[b].set(c_ctx)
    assert lb_logits.shape == (2, 2, HGRN_WIDTH)
    lbl = lb_logits.reshape(4, HGRN_WIDTH)
    row = lambda a: a.reshape(1, -1)
    w_in_bf = w_in.astype(BF16)

    mod = _adaln(cc, w_mod, b_mod)
    s0f, s0b = _ctx_states(ctx, row(norm1_g), mod, lbl, w_in_bf)
    q, v, kf, gf, kb, gb, og, p, ga, gbm, safe = _in_proj(x, row(norm1_g), mod, lbl, w_in_bf)
    per_tile = TOKEN_TILE // HGRN_CHUNK
    safe = safe[:, :, :2 * per_tile, 0].reshape(b, -1, 2, per_tile)
    flags = safe.transpose(0, 1, 3, 2).astype(I32).reshape(-1)
    o_f, o_b = _hgrn(flags, q, v, kf, gf, kb, gb, s0f, s0b)
    pooled = _pool(p, w_pool.astype(BF16), pool_scale)
    new_hx, vx, probs_t = _merge(x, o_f, o_b, og, pooled, ga, gbm, mod, row(hgrn_norm_g), row(norm2_g),
                                 w_a.astype(BF16), w_b.astype(BF16), w_out.astype(BF16),
                                 w_router.T.astype(BF16))
    code, starts = _route(probs_t, cap)
    starts_flat = starts.reshape(-1)
    xg = _dispatch(starts_flat, vx, code, cap)
    y = _moe(xg, w_e_gate, w_e_up, w_e_down, cap)
    return _combine(starts_flat, new_hx, code, probs_t, mod, row(final_g), y, cap)


def kernel(x, c, ctx, c_ctx, lb_logits, w_mod, b_mod, norm1_g, w_in, hgrn_norm_g, w_a, w_pool, pool_scale,
           w_b, w_out, norm2_g, w_router, w_e_gate, w_e_up, w_e_down, final_g):
    assert w_mod.shape[0] == 1, "single-layer trunk: the context stream only seeds the latent recurrence"
    return _layer(x, c, ctx, c_ctx, lb_logits, w_mod[0], b_mod[0], norm1_g[0], w_in[0], hgrn_norm_g[0], w_a[0],
                  w_pool[0], pool_scale[0], w_b[0], w_out[0], norm2_g[0], w_router[0], w_e_gate[0],
                  w_e_up[0], w_e_down[0], final_g)
```

```python
import functools

import numpy as np
import jax
import jax.numpy as jnp
from jax import lax
from jax.experimental import pallas as pl
from jax.experimental.pallas import tpu as pltpu

F32 = jnp.float32
BF16 = jnp.bfloat16
I32 = jnp.int32

D_MODEL = 1024
N_HEADS = 8
HEAD_DIM = 128
HGRN_WIDTH = N_HEADS * HEAD_DIM
GRID_W = 64
POOL_WINDOWS = (2, 4, 8, 16)
POOL_GROUP = 128
POOL_WIDTH = POOL_GROUP * len(POOL_WINDOWS)
N_EXPERTS = 16
CAPACITY_FACTOR = 2
EXPERT_FF = 1024
EPS = 1e-6

Q_OFF = 0
I_OFF = Q_OFF + HGRN_WIDTH
FF_OFF = I_OFF + HGRN_WIDTH
FB_OFF = FF_OFF + HGRN_WIDTH
OG_OFF = FB_OFF + HGRN_WIDTH
P_OFF = OG_OFF + HGRN_WIDTH
GA_OFF = P_OFF + POOL_WIDTH
GB_OFF = GA_OFF + D_MODEL
IN_COLS = GB_OFF + D_MODEL

LANES = 128
SUBLANES = 8
SUBLANE_SHIFT = 3
MOD_ROWS = 16
ADALN_TILE = 768
PROJ_TILE = 512
MERGE_ROWS = 512
MERGE_AHEAD = 2
TOKEN_TILE = 256
HGRN_CHUNK = 128
HGRN_BLOCK = 32
HGRN_STEP_CHUNKS = 4
HGRN_SAFE_DECAY = 80.0
POOL_HALO = 16
POOL_TILE = 256
POOL_UNROLL = 8
SLOT_SHIFT = 6
SLOT_WINDOW = 1 << SLOT_SHIFT
FFN_ROWS = 256
COMBINE_AHEAD = 2
VMEM_LIMIT = 56 * 1024 * 1024


def _dot(a, b):
    return jnp.dot(a, b, preferred_element_type=F32)


def _dot_nt(a, b):
    return lax.dot_general(a, b, (((1,), (1,)), ((), ())), preferred_element_type=F32)


def _dot_tn(a, b):
    return lax.dot_general(a, b, (((0,), (0,)), ((), ())), preferred_element_type=F32)


def _rms(x, g):
    ms = jnp.mean(x * x, axis=-1, keepdims=True)
    return x * lax.rsqrt(ms + EPS) * g


def _silu(z):
    return z * jax.nn.sigmoid(z)


def _lower_bound(a0, a1):
    m = jnp.maximum(a0, a1)
    e0 = jnp.exp(a0 - m)
    e1 = jnp.exp(a1 - m)
    return e0 / (e0 + e1)


def _split3(g):
    hi = g.astype(BF16)
    r = g - hi.astype(F32)
    mid = r.astype(BF16)
    lo = (r - mid.astype(F32)).astype(BF16)
    return hi, mid, lo


def _apply01(u01, g):
    hi, mid, lo = _split3(g)
    return _dot(u01, hi) + _dot(u01, mid) + _dot(u01, lo)


def _one_zero(mask):
    return jnp.where(mask, 1.0, 0.0).astype(BF16)


def _params(sem):
    return pltpu.CompilerParams(dimension_semantics=sem, vmem_limit_bytes=VMEM_LIMIT)


def _resident(shape, index_map):
    return pl.BlockSpec(shape, index_map, pipeline_mode=pl.Buffered(1))


def _adaln_kernel(c_ref, w_ref, b_ref, o_ref):
    c = c_ref[...]
    o_ref[...] = _dot(_silu(c).astype(BF16), w_ref[...].astype(BF16)) + b_ref[...]


def _adaln(cc, w_mod, b_mod):
    n = w_mod.shape[1]
    tn = ADALN_TILE
    assert n % tn == 0
    return pl.pallas_call(
        _adaln_kernel,
        grid=(n // tn,),
        in_specs=[pl.BlockSpec((MOD_ROWS, D_MODEL), lambda j: (0, 0)),
                  pl.BlockSpec((D_MODEL, tn), lambda j: (0, j)),
                  pl.BlockSpec((1, tn), lambda j: (0, j))],
        out_specs=pl.BlockSpec((MOD_ROWS, tn), lambda j: (0, j)),
        out_shape=jax.ShapeDtypeStruct((MOD_ROWS, n), F32),
        compiler_params=_params(("arbitrary",)),
        name="adaln",
    )(cc, w_mod, b_mod.reshape(1, n))


def _ctx_kernel(ctx_ref, g1_ref, mod_ref, lbl_ref, wi_ref, wf_ref, wb_ref, sf_ref, sb_ref, *, ctx_row):
    x = ctx_ref[0]
    n = x.shape[0]
    sh = mod_ref[ctx_row:ctx_row + 1, 0:D_MODEL]
    sc = mod_ref[ctx_row:ctx_row + 1, D_MODEL:2 * D_MODEL]
    u = (_rms(x, g1_ref[...]) * (1.0 + sc) + sh).astype(BF16)
    v = _dot(u, wi_ref[...]).astype(BF16)
    zf = _dot(u, wf_ref[...])
    zb = _dot(u, wb_ref[...])
    lbf = _lower_bound(lbl_ref[0:1, :], lbl_ref[1:2, :])
    lbb = _lower_bound(lbl_ref[2:3, :], lbl_ref[3:4, :])
    ff = lbf + (1.0 - lbf) * jax.nn.sigmoid(zf)
    fb = lbb + (1.0 - lbb) * jax.nn.sigmoid(zb)
    r = lax.broadcasted_iota(I32, (n, n), 0)
    c = lax.broadcasted_iota(I32, (n, n), 1)
    ef = _apply01(_one_zero(c > r), jnp.log(ff))
    eb = _apply01(_one_zero(c < r), jnp.log(fb))
    kf = ((1.0 - ff) * jnp.exp(ef)).astype(BF16)
    kb = ((1.0 - fb) * jnp.exp(eb)).astype(BF16)
    for h in range(N_HEADS):
        hs = slice(h * HEAD_DIM, (h + 1) * HEAD_DIM)
        sf_ref[0, h] = _dot_tn(kf[:, hs], v[:, hs])
        sb_ref[0, h] = _dot_tn(kb[:, hs], v[:, hs])


def _ctx_states(ctx, norm1_g, mod, lbl, w_in_bf):
    b, lc, d = ctx.shape
    col = lambda k: pl.BlockSpec((d, HGRN_WIDTH), lambda i, k=k: (0, k))
    state = jax.ShapeDtypeStruct((b, N_HEADS, HEAD_DIM, HEAD_DIM), F32)
    sspec = pl.BlockSpec((1, N_HEADS, HEAD_DIM, HEAD_DIM), lambda i: (i, 0, 0, 0))
    return pl.pallas_call(
        functools.partial(_ctx_kernel, ctx_row=b),
        grid=(b,),
        in_specs=[pl.BlockSpec((1, lc, d), lambda i: (i, 0, 0)),
                  pl.BlockSpec((1, d), lambda i: (0, 0)),
                  pl.BlockSpec(mod.shape, lambda i: (0, 0)),
                  pl.BlockSpec(lbl.shape, lambda i: (0, 0)),
                  col(I_OFF // HGRN_WIDTH), col(FF_OFF // HGRN_WIDTH), col(FB_OFF // HGRN_WIDTH)],
        out_specs=[sspec, sspec],
        out_shape=[state, state],
        compiler_params=_params(("arbitrary",)),
        name="ctx_state",
    )(ctx, norm1_g, mod, lbl, w_in_bf, w_in_bf, w_in_bf)


def _chunks_safe(logf):
    out = []
    for c0 in range(0, logf.shape[0], HGRN_CHUNK):
        worst = None
        for lo in range(c0, c0 + HGRN_CHUNK, HGRN_BLOCK):
            dec = jnp.sum(logf[lo:lo + HGRN_BLOCK], axis=0, keepdims=True)
            worst = dec if worst is None else jnp.minimum(worst, dec)
        ok = jnp.min(worst, axis=1, keepdims=True) >= -HGRN_SAFE_DECAY
        out.append(jnp.broadcast_to(jnp.where(ok, 1.0, 0.0), (1, LANES)))
    return out


def _inproj_kernel(x_ref, g1_ref, mod_ref, lbl_ref, w_ref,
                   q_ref, v_ref, kf_ref, gf_ref, kb_ref, gb_ref, og_ref, p_ref, ga_ref, gbm_ref, safe_ref):
    b = pl.program_id(0)
    x = x_ref[0]
    m = mod_ref[pl.ds(b, 1), :]
    sh = m[:, 0:D_MODEL]
    sc = m[:, D_MODEL:2 * D_MODEL]
    u = (_rms(x, g1_ref[...]) * (1.0 + sc) + sh).astype(BF16)

    z = _dot(u, w_ref[:, Q_OFF:I_OFF])
    q_ref[0] = _silu(z).astype(BF16)

    lbf = _lower_bound(lbl_ref[0:1, :], lbl_ref[1:2, :])
    f = lbf + (1.0 - lbf) * jax.nn.sigmoid(_dot(u, w_ref[:, FF_OFF:FB_OFF]))
    kf_ref[0] = (1.0 - f).astype(BF16)
    logf = jnp.log(f)
    gf_ref[0] = logf
    flags = _chunks_safe(logf)
    lbb = _lower_bound(lbl_ref[2:3, :], lbl_ref[3:4, :])
    f = lbb + (1.0 - lbb) * jax.nn.sigmoid(_dot(u, w_ref[:, FB_OFF:OG_OFF]))
    kb_ref[0] = (1.0 - f).astype(BF16)
    logf = jnp.log(f)
    gb_ref[0] = logf
    flags = flags + _chunks_safe(logf)
    safe_ref[0, 0] = jnp.concatenate(flags + [jnp.zeros((SUBLANES - len(flags), LANES), F32)], axis=0)

    z = _dot(u, w_ref[:, OG_OFF:P_OFF])
    og_ref[0] = _silu(z).astype(BF16)
    ga_ref[0] = jax.nn.sigmoid(_dot(u, w_ref[:, GA_OFF:GB_OFF])).astype(BF16)
    gbm_ref[0] = jax.nn.sigmoid(_dot(u, w_ref[:, GB_OFF:IN_COLS])).astype(BF16)
    p_ref[0] = _dot(u, w_ref[:, P_OFF:GA_OFF])
    v_ref[0] = _dot(u, w_ref[:, I_OFF:FF_OFF]).astype(BF16)


def _in_proj(x, norm1_g, mod, lbl, w_in_bf):
    b, l, d = x.shape
    tm = TOKEN_TILE
    tok = lambda w: pl.BlockSpec((1, tm, w), lambda i, j: (i, j, 0))
    wide = lambda dt, w=HGRN_WIDTH: jax.ShapeDtypeStruct((b, l, w), dt)
    return pl.pallas_call(
        _inproj_kernel,
        grid=(b, l // tm),
        in_specs=[tok(d),
                  pl.BlockSpec((1, d), lambda i, j: (0, 0)),
                  pl.BlockSpec(mod.shape, lambda i, j: (0, 0)),
                  pl.BlockSpec(lbl.shape, lambda i, j: (0, 0)),
                  _resident((d, IN_COLS), lambda i, j: (0, 0))],
        out_specs=[tok(HGRN_WIDTH)] * 7 + [tok(POOL_WIDTH), tok(d), tok(d),
                                            pl.BlockSpec((1, 1, SUBLANES, LANES), lambda i, j: (i, j, 0, 0))],
        out_shape=[wide(BF16), wide(BF16), wide(BF16), wide(F32), wide(BF16), wide(F32), wide(BF16),
                   wide(F32, POOL_WIDTH), wide(BF16, d), wide(BF16, d),
                   jax.ShapeDtypeStruct((b, l // tm, SUBLANES, LANES), F32)],
        compiler_params=_params(("arbitrary", "arbitrary")),
        name="in_proj",
    )(x, norm1_g, mod, lbl, w_in_bf)


def _reads(i, j, reverse):
    return i <= j if reverse else i >= j


def _block_edges(b_ref, reverse):
    nb = HGRN_CHUNK // HGRN_BLOCK
    zero = jnp.zeros((1, HGRN_WIDTH), F32)
    out = []
    for j in range(nb):
        if reverse:
            r = (j + 1) * HGRN_BLOCK
            out.append(zero if j == nb - 1 else b_ref[r:r + 1, :])
        else:
            r = j * HGRN_BLOCK
            out.append(zero if j == 0 else b_ref[r - 1:r, :])
    return out


def _seen(reverse):
    t = HGRN_CHUNK
    row = lax.broadcasted_iota(I32, (t, t), 0)
    col = lax.broadcasted_iota(I32, (t, t), 1)
    return (col >= row) if reverse else (col <= row)


def _hgrn_decay(g_ref, b_ref, reverse):
    g = g_ref[...]
    hi = g.astype(BF16)
    lo = (g - hi.astype(F32)).astype(BF16)
    u01 = _one_zero(_seen(reverse))
    b_ref[...] = _dot(u01, hi) + _dot(u01, lo)


def _chunk_end(b_ref, reverse):
    return b_ref[0:1, :] if reverse else b_ref[HGRN_CHUNK - 1:HGRN_CHUNK, :]


def _decay_columns(b_end):
    e_end = jnp.exp(b_end)
    rows = [e_end[:, h * HEAD_DIM:(h + 1) * HEAD_DIM] for h in range(N_HEADS)]
    return jnp.concatenate(rows + [jnp.zeros((HEAD_DIM - N_HEADS, HEAD_DIM), F32)], axis=0).T


def _hgrn_fast(q_ref, v_ref, k_ref, b_ref, o_ref, s_ref, reverse):
    t, bl = HGRN_CHUNK, HGRN_BLOCK
    nb = t // bl
    seen = _seen(reverse)
    edges = _block_edges(b_ref, reverse)
    gain = {(i, j): jnp.exp(edges[i] - edges[j])
            for i in range(nb) for j in range(nb) if i != j and _reads(i, j, reverse)}
    b_end = _chunk_end(b_ref, reverse)
    e_cols = _decay_columns(b_end)
    zblk = jnp.zeros((bl, HEAD_DIM), BF16)
    full = nb - 1 if reverse else 0
    heads = [slice(h * HEAD_DIM, (h + 1) * HEAD_DIM) for h in range(N_HEADS)]

    scores, q_full, k_state = [], [], []
    for hs in heads:
        binc = b_ref[:, hs]
        q = q_ref[:, hs].astype(F32)
        k = k_ref[:, hs].astype(F32)
        q_own, k_own = [], []
        for i in range(nb):
            bs = slice(i * bl, (i + 1) * bl)
            q_own.append(q[bs] * jnp.exp(binc[bs] - edges[i][:, hs]))
            k_own.append((k[bs] * jnp.exp(edges[i][:, hs] - binc[bs])).astype(BF16))
        q_slabs = []
        for j in range(nb):
            parts = []
            for i in range(nb):
                if not _reads(i, j, reverse):
                    parts.append(zblk)
                elif i == j:
                    parts.append(q_own[i].astype(BF16))
                else:
                    parts.append((q_own[i] * gain[i, j][:, hs]).astype(BF16))
            q_slabs.append(jnp.concatenate(parts, axis=0))
        k_slabs = [jnp.concatenate([k_own[i] if i == j else zblk for i in range(nb)], axis=0)
                   for j in range(nb)]
        near = slice(nb // 2, nb) if reverse else slice(0, nb // 2)
        near_rows = slice(t // 2, t) if reverse else slice(0, t // 2)
        far_rows = slice(0, t // 2) if reverse else slice(t // 2, t)
        s_near = _dot_nt(jnp.concatenate([s[near_rows] for s in q_slabs[near]], axis=1),
                         jnp.concatenate(k_slabs[near], axis=1))
        s_far = _dot_nt(jnp.concatenate([s[far_rows] for s in q_slabs], axis=1),
                        jnp.concatenate(k_slabs, axis=1))
        scores.append(jnp.concatenate([s_far, s_near] if reverse else [s_near, s_far], axis=0))
        q_full.append(q_slabs[full])
        k_state.append((k * jnp.exp(b_end[:, hs] - binc)).astype(BF16))

    def readout():
        for h, hs in enumerate(heads):
            p = jnp.where(seen, scores[h], 0.0).astype(BF16)
            lhs = jnp.concatenate([p, q_full[h]], axis=1)
            rhs = jnp.concatenate([v_ref[:, hs], s_ref[h].astype(BF16)], axis=0)
            o_ref[:, hs] = _dot(lhs, rhs).astype(o_ref.dtype)

    def update():
        for h, hs in enumerate(heads):
            s_ref[h] = s_ref[h] * e_cols[:, h:h + 1] + _dot_tn(k_state[h], v_ref[:, hs])

    return readout, update


def _hgrn_slow(q_ref, v_ref, k_ref, b_ref, o_ref, s_ref, oi_ref, kk_ref, vv_ref, reverse):
    t = HGRN_CHUNK
    q = q_ref[...].astype(F32)
    binc = b_ref[...]
    kk_ref[...] = k_ref[...].astype(F32)
    vv_ref[...] = v_ref[...].astype(F32)
    oi_ref[...] = jnp.zeros((t, HGRN_WIDTH), F32)
    trow = lax.broadcasted_iota(I32, (t, HGRN_WIDTH), 0)
    heads = [slice(h * HEAD_DIM, (h + 1) * HEAD_DIM) for h in range(N_HEADS)]

    def body(s, carry):
        reads = (trow <= s) if reverse else (trow >= s)
        w = jnp.exp(jnp.where(reads, binc - b_ref[pl.ds(s, 1), :], 0.0))
        prod = jnp.where(reads, q * kk_ref[pl.ds(s, 1), :] * w, 0.0)
        vs = vv_ref[pl.ds(s, 1), :]
        for hs in heads:
            oi_ref[:, hs] += jnp.sum(prod[:, hs], axis=1, keepdims=True) * vs[:, hs]
        return carry

    lax.fori_loop(0, t, body, 0)
    b_end = _chunk_end(b_ref, reverse)
    e_cols = _decay_columns(b_end)
    q_full = (q * jnp.exp(binc)).astype(BF16)
    k_state = (kk_ref[...] * jnp.exp(b_end - binc)).astype(BF16)
    for h, hs in enumerate(heads):
        o_ref[:, hs] = (oi_ref[:, hs] + _dot(q_full[:, hs], s_ref[h].astype(BF16))).astype(o_ref.dtype)
        s_ref[h] = s_ref[h] * e_cols[:, h:h + 1] + _dot_tn(k_state[:, hs], v_ref[:, hs])


def _hgrn_kernel(flag_ref, qf_ref, vf_ref, kf_ref, gf_ref, qb_ref, vb_ref, kb_ref, gb_ref, s0f_ref, s0b_ref,
                 of_ref, ob_ref, sf_ref, sb_ref, bf_ref, bb_ref, oi_ref, kk_ref, vv_ref):
    b = pl.program_id(0)
    j = pl.program_id(1)
    n = pl.num_programs(1) * HGRN_STEP_CHUNKS
    t = HGRN_CHUNK

    @pl.when(j == 0)
    def _():
        sf_ref[...] = s0f_ref[0]
        sb_ref[...] = s0b_ref[0]

    def chunk(s, carry):
        rf = pl.ds(pl.multiple_of(s * t, t), t)
        rb = pl.ds(pl.multiple_of((HGRN_STEP_CHUNKS - 1 - s) * t, t), t)
        qf, vf, kf, gf, of = (r.at[0, rf, :] for r in (qf_ref, vf_ref, kf_ref, gf_ref, of_ref))
        qb, vb, kb, gb, ob = (r.at[0, rb, :] for r in (qb_ref, vb_ref, kb_ref, gb_ref, ob_ref))
        c = j * HGRN_STEP_CHUNKS + s
        safe = (flag_ref[(b * n + c) * 2] + flag_ref[(b * n + n - 1 - c) * 2 + 1]) == 2

        @pl.when(safe)
        def _():
            _hgrn_decay(gf, bf_ref, False)
            _hgrn_decay(gb, bb_ref, True)
            read_f, update_f = _hgrn_fast(qf, vf, kf, bf_ref, of, sf_ref, False)
            read_b, update_b = _hgrn_fast(qb, vb, kb, bb_ref, ob, sb_ref, True)
            read_f()
            read_b()
            update_f()
            update_b()

        @pl.when(jnp.logical_not(safe))
        def _():
            _hgrn_decay(gf, bf_ref, False)
            _hgrn_decay(gb, bb_ref, True)
            _hgrn_slow(qf, vf, kf, bf_ref, of, sf_ref, oi_ref, kk_ref, vv_ref, False)
            _hgrn_slow(qb, vb, kb, bb_ref, ob, sb_ref, oi_ref, kk_ref, vv_ref, True)

        return carry

    lax.fori_loop(0, HGRN_STEP_CHUNKS, chunk, 0)


def _hgrn(flags, q, v, kf, gf, kb, gb, s0f, s0b):
    b, l, w = q.shape
    t = HGRN_CHUNK
    rows_per_step = t * HGRN_STEP_CHUNKS
    n = l // rows_per_step
    fwd = pl.BlockSpec((1, rows_per_step, w), lambda i, j, f: (i, j, 0))
    bwd = pl.BlockSpec((1, rows_per_step, w), lambda i, j, f: (i, n - 1 - j, 0))
    sspec = pl.BlockSpec((1, N_HEADS, HEAD_DIM, HEAD_DIM), lambda i, j, f: (i, 0, 0, 0))
    out = jax.ShapeDtypeStruct((b, l, w), BF16)
    state = pltpu.VMEM((N_HEADS, HEAD_DIM, HEAD_DIM), F32)
    rows = pltpu.VMEM((t, w), F32)
    grid_spec = pltpu.PrefetchScalarGridSpec(
        num_scalar_prefetch=1,
        grid=(b, n),
        in_specs=[fwd, fwd, fwd, fwd, bwd, bwd, bwd, bwd, sspec, sspec],
        out_specs=[fwd, bwd],
        scratch_shapes=[state, state, rows, rows, rows, rows, rows],
    )
    return pl.pallas_call(
        _hgrn_kernel,
        grid_spec=grid_spec,
        out_shape=[out, out],
        compiler_params=_params(("arbitrary", "arbitrary")),
        name="hgrn",
    )(flags, q, v, kf, gf, q, v, kb, gb, s0f, s0b)


def _pool_col_matrices():
    mats = np.zeros((len(POOL_WINDOWS), POOL_TILE, POOL_TILE), np.float32)
    for j, w in enumerate(POOL_WINDOWS):
        for t in range(POOL_TILE):
            r, c = divmod(t, GRID_W)
            lo, hi = max(c - w // 2, 0), min(c + w // 2 - 1, GRID_W - 1)
            mats[j, t, r * GRID_W + lo:r * GRID_W + hi + 1] = 1.0
    return jnp.asarray(mats, BF16)


def _window_len(pos, half, n):
    return jnp.minimum(pos + half - 1, n - 1) + 1 - jnp.maximum(pos - half, 0)


def _pool_kernel(p_ref, a_ref, wp_ref, ps_ref, o_ref, pad_ref, inr_ref, inc_ref, *, rows):
    j = pl.program_id(1)
    half = jnp.left_shift(1, j)
    gw = GRID_W
    halo = POOL_HALO * gw
    l = rows * gw
    tile_rows = POOL_TILE // gw
    edge = POOL_HALO // 2
    for buf in range(2):
        pad_ref[buf, 0:halo, :] = jnp.zeros((halo, POOL_GROUP), F32)
        pad_ref[buf, halo + l:2 * halo + l, :] = jnp.zeros((halo, POOL_GROUP), F32)
    pad_ref[0, halo:halo + l, :] = p_ref[0]

    def level(src, dst, back, fwd):
        def body(g, carry):
            for u in range(8):
                t = pl.multiple_of((edge + g * 8 + u) * gw, gw)
                pad_ref[dst, pl.ds(t, gw), :] = (pad_ref[src, pl.ds(t - back * gw, gw), :]
                                                 + pad_ref[src, pl.ds(t + fwd * gw, gw), :])
            return carry

        lax.fori_loop(0, (rows + 2 * (POOL_HALO - edge)) // 8, body, 0)

    level(0, 1, 1, 0)

    @pl.when(j >= 1)
    def _():
        level(1, 0, 1, 1)

    @pl.when(j >= 2)
    def _():
        level(0, 1, 2, 2)

    @pl.when(j >= 3)
    def _():
        level(1, 0, 4, 4)

    summed = 1 - jnp.bitwise_and(j, 1)
    r = lax.broadcasted_iota(I32, inr_ref.shape, 0)
    inr_ref[...] = 1.0 / _window_len(r, half, rows).astype(F32)
    c = jnp.bitwise_and(lax.broadcasted_iota(I32, inc_ref.shape, 0), gw - 1)
    inc_ref[...] = 1.0 / _window_len(c, half, gw).astype(F32)
    a = a_ref[0]
    wp = wp_ref[0]
    scale = ps_ref[0]

    def step(i, carry):
        tiles = [POOL_UNROLL * i + u for u in range(POOL_UNROLL)]
        starts = [pl.multiple_of(t * POOL_TILE, POOL_TILE) for t in tiles]
        sums = []
        for t0 in starts:
            rsum = pad_ref[summed, pl.ds(halo + t0, POOL_TILE), :]
            hi = rsum.astype(BF16)
            lo = (rsum - hi.astype(F32)).astype(BF16)
            sums.append(_dot(a, jnp.concatenate([hi, lo], axis=1)))
        resid = []
        for t, t0, both in zip(tiles, starts, sums):
            total = both[:, :POOL_GROUP] + both[:, POOL_GROUP:]
            by_row = [total[rr * gw:(rr + 1) * gw] * inr_ref[pl.ds(t * tile_rows + rr, 1), :]
                      for rr in range(tile_rows)]
            resid.append((jnp.concatenate(by_row, axis=0) * inc_ref[...]
                          - p_ref[0, pl.ds(t0, POOL_TILE), :]).astype(BF16))
        for t0, rb in zip(starts, resid):
            o_ref[0, pl.ds(t0, POOL_TILE), :] = (_dot(rb, wp) * scale).astype(BF16)
        return carry

    lax.fori_loop(0, l // (POOL_UNROLL * POOL_TILE), step, 0)


def _pool(p, w_pool_bf, pool_scale):
    b, l, _ = p.shape
    rows = l // GRID_W
    ng = len(POOL_WINDOWS)
    grp = pl.BlockSpec((1, l, POOL_GROUP), lambda i, j: (i, 0, j))
    return pl.pallas_call(
        functools.partial(_pool_kernel, rows=rows),
        grid=(b, ng),
        in_specs=[grp,
                  pl.BlockSpec((1, POOL_TILE, POOL_TILE), lambda i, j: (j, 0, 0)),
                  pl.BlockSpec((1, POOL_GROUP, POOL_GROUP), lambda i, j: (j, 0, 0)),
                  pl.BlockSpec((1, 1, POOL_GROUP), lambda i, j: (j, 0, 0))],
        out_specs=grp,
        out_shape=jax.ShapeDtypeStruct((b, l, POOL_WIDTH), BF16),
        scratch_shapes=[pltpu.VMEM((2, (rows + 2 * POOL_HALO) * GRID_W, POOL_GROUP), F32),
                        pltpu.VMEM((rows, POOL_GROUP), F32),
                        pltpu.VMEM((POOL_TILE, POOL_GROUP), F32)],
        compiler_params=_params(("arbitrary", "arbitrary")),
        name="pool",
    )(p, _pool_col_matrices(), w_pool_bf, pool_scale.reshape(ng, 1, POOL_GROUP))


def _merge_kernel(x_hbm, of_hbm, ob_hbm, og_hbm, pl_hbm, ga_hbm, gb_hbm, mod_ref, hg_ref, n2_ref,
                  wa_ref, wb_ref, wo_ref, wr_ref, nh_ref, vx_ref, pr_ref,
                  x_ring, of_ring, ob_ring, og_ring, pl_ring, ga_ring, gb_ring, sem):
    b = pl.program_id(0)
    j = pl.program_id(1)
    nj = pl.num_programs(1)
    step = b * nj + j
    n_steps = pl.num_programs(0) * nj
    tm = x_ring.shape[1]
    streams = ((x_hbm, x_ring), (of_hbm, of_ring), (ob_hbm, ob_ring), (og_hbm, og_ring),
               (pl_hbm, pl_ring), (ga_hbm, ga_ring), (gb_hbm, gb_ring))

    def copies(ahead, slot):
        jj = j + ahead
        wrap = jj >= nj
        bb = jnp.where(wrap, b + 1, b)
        rows = pl.ds(pl.multiple_of(jnp.where(wrap, jj - nj, jj) * tm, tm), tm)
        return [pltpu.make_async_copy(src.at[bb, rows, :], ring.at[slot], sem.at[slot, n])
                for n, (src, ring) in enumerate(streams)]

    @pl.when(step == 0)
    def _():
        for a in range(MERGE_AHEAD):
            for cp in copies(a, a):
                cp.start()

    cur = lax.rem(step, MERGE_AHEAD + 1)

    @pl.when(step + MERGE_AHEAD < n_steps)
    def _():
        for cp in copies(MERGE_AHEAD, lax.rem(step + MERGE_AHEAD, MERGE_AHEAD + 1)):
            cp.start()

    for cp in copies(0, cur):
        cp.wait()

    m = mod_ref[pl.ds(b, 1), :]
    g1 = m[:, 2 * D_MODEL:3 * D_MODEL]
    sh2 = m[:, 3 * D_MODEL:4 * D_MODEL]
    sc2 = m[:, 4 * D_MODEL:5 * D_MODEL]
    hg = hg_ref[...]
    for r0 in range(0, tm, MERGE_ROWS):
        rs = slice(r0, r0 + MERGE_ROWS)
        gated = []
        for h in range(N_HEADS):
            hs = slice(h * HEAD_DIM, (h + 1) * HEAD_DIM)
            o = of_ring[cur, rs, hs].astype(F32) + ob_ring[cur, rs, hs].astype(F32)
            gated.append((_rms(o, hg) * og_ring[cur, rs, hs].astype(F32)).astype(BF16))
        ya = _dot(jnp.concatenate(gated, axis=1), wa_ref[...])
        yb = _dot(pl_ring[cur, rs, :], wb_ref[...])
        y = ga_ring[cur, rs, :].astype(F32) * ya + gb_ring[cur, rs, :].astype(F32) * yb
        nh = x_ring[cur, rs, :] + g1 * _dot(y.astype(BF16), wo_ref[...])
        nh_ref[0, rs, :] = nh
        vx = (_rms(nh, n2_ref[...]) * (1.0 + sc2) + sh2).astype(BF16)
        vx_ref[0, rs, :] = vx
        logits = _dot_nt(wr_ref[...], vx)
        ex = jnp.exp(logits - jnp.max(logits, axis=0, keepdims=True))
        pr_ref[0, :, rs] = ex / jnp.sum(ex, axis=0, keepdims=True)


def _merge(x, o_f, o_b, og, pooled, ga, gb, mod, hgrn_g, norm2_g, w_a_bf, w_b_bf, w_out_bf, w_router_t):
    b, l, d = x.shape
    tm = PROJ_TILE
    tok = lambda w: pl.BlockSpec((1, tm, w), lambda i, j: (i, j, 0))
    const = lambda a: pl.BlockSpec(a.shape, lambda i, j: (0,) * a.ndim)
    hbm = pl.BlockSpec(memory_space=pl.ANY)
    ring = lambda w, dt: pltpu.VMEM((MERGE_AHEAD + 1, tm, w), dt)
    assert b * (l // tm) >= MERGE_AHEAD
    return pl.pallas_call(
        _merge_kernel,
        grid=(b, l // tm),
        in_specs=[hbm] * 7 + [
                  const(mod), const(hgrn_g), const(norm2_g),
                  _resident(w_a_bf.shape, lambda i, j: (0, 0)),
                  _resident(w_b_bf.shape, lambda i, j: (0, 0)),
                  _resident(w_out_bf.shape, lambda i, j: (0, 0)),
                  const(w_router_t)],
        out_specs=[tok(d), tok(d), pl.BlockSpec((1, N_EXPERTS, tm), lambda i, j: (i, 0, j))],
        out_shape=[jax.ShapeDtypeStruct((b, l, d), F32),
                   jax.ShapeDtypeStruct((b, l, d), BF16),
                   jax.ShapeDtypeStruct((b, N_EXPERTS, l), F32)],
        scratch_shapes=[ring(d, F32), ring(d, BF16), ring(d, BF16), ring(d, BF16), ring(POOL_WIDTH, BF16),
                        ring(d, BF16), ring(d, BF16), pltpu.SemaphoreType.DMA((MERGE_AHEAD + 1, 7))],
        compiler_params=_params(("arbitrary", "arbitrary")),
        name="merge",
    )(x, o_f, o_b, og, pooled, ga, gb, mod, hgrn_g, norm2_g, w_a_bf, w_b_bf, w_out_bf, w_router_t)


def _route_kernel(p_ref, code_ref, start_ref, *, cap):
    l = p_ref.shape[2]
    tm = TOKEN_TILE
    bits = lax.bitcast_convert_type(p_ref[0], I32)
    capf = jnp.float32(cap)

    def count_ge(cand):
        return jnp.sum(jnp.where(bits >= cand, 1.0, 0.0), axis=1, keepdims=True)

    def search(i, prefix):
        cand = jnp.bitwise_or(prefix, jnp.left_shift(1, 30 - i))
        return jnp.where(count_ge(cand) >= capf, cand, prefix)

    tau = lax.fori_loop(0, 31, search, jnp.zeros((N_EXPERTS, 1), I32))
    need = capf - jnp.sum(jnp.where(bits > tau, 1.0, 0.0), axis=1, keepdims=True)
    before = _one_zero(lax.broadcasted_iota(I32, (tm, tm), 0) < lax.broadcasted_iota(I32, (tm, tm), 1))
    lane = lax.broadcasted_iota(I32, (N_EXPERTS, LANES), 1)

    def tile(k, carry):
        n_gt, n_eq, starts = carry
        sl = pl.ds(pl.multiple_of(k * tm, tm), tm)
        bk = lax.bitcast_convert_type(p_ref[0, :, sl], I32)
        gt = bk > tau
        eq = bk == tau
        gt01 = _one_zero(gt)
        eq01 = _one_zero(eq)
        eq_before = n_eq + _dot(eq01, before)
        chosen = jnp.logical_or(gt, jnp.logical_and(eq, eq_before < need))
        rank = n_gt + _dot(gt01, before) + jnp.minimum(eq_before, need)
        code_ref[0, :, sl] = jnp.where(chosen, rank, -1.0).astype(I32)
        starts = jnp.where(lane == k, (n_gt + jnp.minimum(n_eq, need)).astype(I32), starts)
        n_gt = n_gt + jnp.sum(gt01.astype(F32), axis=1, keepdims=True)
        n_eq = n_eq + jnp.sum(eq01.astype(F32), axis=1, keepdims=True)
        return n_gt, n_eq, starts

    zero = jnp.zeros((N_EXPERTS, 1), F32)
    _, _, starts = lax.fori_loop(0, l // tm, tile, (zero, zero, jnp.zeros((N_EXPERTS, LANES), I32)))
    start_ref[0] = jnp.where(lane == l // tm, cap, starts)


def _route(probs_t, cap):
    b, e, l = probs_t.shape
    assert l // TOKEN_TILE < LANES
    return pl.pallas_call(
        functools.partial(_route_kernel, cap=cap),
        grid=(b,),
        in_specs=[pl.BlockSpec((1, e, l), lambda i: (i, 0, 0))],
        out_specs=[pl.BlockSpec((1, e, l), lambda i: (i, 0, 0)),
                   pl.BlockSpec((1, e, LANES), lambda i: (i, 0, 0))],
        out_shape=[jax.ShapeDtypeStruct((b, e, l), I32), jax.ShapeDtypeStruct((b, e, LANES), I32)],
        compiler_params=_params(("arbitrary",)),
        name="route",
    )(probs_t)


def _floor8(n):
    return jnp.left_shift(jnp.right_shift(n, SUBLANE_SHIFT), SUBLANE_SHIFT)


def _dispatch_kernel(start_ref, vx_ref, code_ref, xg_hbm, stage_ref, carry_ref, sent_ref, sem,
                     *, cap, n_tiles, n_steps):
    b = pl.program_id(0)
    k = pl.program_id(1)
    step = b * n_tiles + k
    w = SLOT_WINDOW
    tm = TOKEN_TILE
    ne = N_EXPERTS

    @pl.when(step == 0)
    def _():
        sent_ref[0] = 0

    @pl.when(k == 0)
    def _():
        carry_ref[...] = jnp.zeros(carry_ref.shape, F32)

    base, tail = [], []
    n_pass = jnp.int32(1)
    for e in range(ne):
        i0 = (b * ne + e) * LANES + k
        end = start_ref[i0 + 1]
        base.append(_floor8(start_ref[i0]))
        tail.append(_floor8(end))
        n_pass = jnp.maximum(n_pass, jnp.right_shift(end - base[e] + w - 1, SLOT_SHIFT))

    def send(buf, rows):
        @pl.when(sent_ref[0] > 0)
        def _():
            for e in range(ne):
                pltpu.make_async_copy(stage_ref.at[0, pl.ds(e * w, w), :], xg_hbm.at[b, e, pl.ds(0, w), :],
                                      sem.at[e]).wait()

        for e in range(ne):
            pltpu.make_async_copy(stage_ref.at[buf, pl.ds(e * w, w), :],
                                  xg_hbm.at[b, e, pl.ds(pl.multiple_of(rows[e], SUBLANES), w), :],
                                  sem.at[e]).start(priority=e % 2)
        sent_ref[0] = sent_ref[0] + 1

    slot = lax.broadcasted_iota(I32, (w, tm), 0)
    g8 = SUBLANES
    group = lax.broadcasted_iota(I32, (g8, tm), 0)

    def one_pass(c, carry):
        buf = jnp.bitwise_and(sent_ref[0], 1)
        hits = [_one_zero(code_ref[0, e:e + 1, :] == (slot + (base[e] + c * w))) for e in range(ne)]
        tails = [_one_zero(code_ref[0, e:e + 1, :] == (group + tail[e])) for e in range(ne)]
        rows = _dot(jnp.concatenate(hits + tails, axis=0), vx_ref[0])
        stage_ref[buf] = rows[:ne * w].astype(BF16)

        @pl.when(c == 0)
        def _():
            for e in range(ne):
                first = rows[e * w:e * w + g8] + carry_ref[e]
                stage_ref[buf, e * w:e * w + g8, :] = first.astype(BF16)
                own = rows[ne * w + g8 * e:ne * w + g8 * (e + 1)]
                carry_ref[e] = jnp.where(tail[e] == base[e], carry_ref[e], 0.0) + own

        send(buf, [jnp.minimum(base[e] + c * w, cap) for e in range(ne)])
        return carry

    lax.fori_loop(0, n_pass, one_pass, 0)

    @pl.when(k == n_tiles - 1)
    def _():
        buf = jnp.bitwise_and(sent_ref[0], 1)
        stage_ref[buf] = jnp.zeros(stage_ref.shape[1:], BF16)
        send(buf, [cap] * ne)

    @pl.when(step == n_steps - 1)
    def _():
        for e in range(ne):
            pltpu.make_async_copy(stage_ref.at[0, pl.ds(e * w, w), :], xg_hbm.at[b, e, pl.ds(0, w), :],
                                  sem.at[e]).wait()


def _dispatch(starts_flat, vx, code, cap):
    b, l, d = vx.shape
    tm = TOKEN_TILE
    n_tiles = l // tm
    grid_spec = pltpu.PrefetchScalarGridSpec(
        num_scalar_prefetch=1,
        grid=(b, n_tiles),
        in_specs=[pl.BlockSpec((1, tm, d), lambda i, j, s: (i, j, 0)),
                  pl.BlockSpec((1, N_EXPERTS, tm), lambda i, j, s: (i, 0, j))],
        out_specs=pl.BlockSpec(memory_space=pl.ANY),
        scratch_shapes=[pltpu.VMEM((2, N_EXPERTS * SLOT_WINDOW, d), BF16),
                        pltpu.VMEM((N_EXPERTS, SUBLANES, d), F32),
                        pltpu.SMEM((1,), I32),
                        pltpu.SemaphoreType.DMA((N_EXPERTS,))],
    )
    return pl.pallas_call(
        functools.partial(_dispatch_kernel, cap=cap, n_tiles=n_tiles, n_steps=b * n_tiles),
        grid_spec=grid_spec,
        out_shape=jax.ShapeDtypeStruct((b, N_EXPERTS, cap + SLOT_WINDOW, d), BF16),
        compiler_params=_params(("arbitrary", "arbitrary")),
        name="dispatch",
    )(starts_flat, vx, code)


def _moe_kernel(xg_ref, wg_ref, wu_ref, wd_ref, y_ref, wgb_ref, wub_ref, wdb_ref, *, cap):
    @pl.when(pl.program_id(1) == 0)
    def _():
        wgb_ref[...] = wg_ref[0].astype(BF16)
        wub_ref[...] = wu_ref[0].astype(BF16)
        wdb_ref[...] = wd_ref[0].astype(BF16)

    fr = min(FFN_ROWS, cap)
    for r0 in range(0, cap, fr):
        xg = xg_ref[0, 0, r0:r0 + fr, :]
        hid = (_silu(_dot(xg, wgb_ref[...])) * _dot(xg, wub_ref[...])).astype(BF16)
        y_ref[0, 0, r0:r0 + fr, :] = _dot(hid, wdb_ref[...]).astype(BF16)


def _moe(xg, w_gate, w_up, w_down, cap):
    b, ne, _, d = xg.shape
    ff = w_gate.shape[2]
    rows = pl.BlockSpec((1, 1, cap, d), lambda e, i: (i, e, 0, 0))
    return pl.pallas_call(
        functools.partial(_moe_kernel, cap=cap),
        grid=(ne, b),
        in_specs=[rows,
                  pl.BlockSpec((1, d, ff), lambda e, i: (e, 0, 0)),
                  pl.BlockSpec((1, d, ff), lambda e, i: (e, 0, 0)),
                  pl.BlockSpec((1, ff, d), lambda e, i: (e, 0, 0))],
        out_specs=rows,
        out_shape=jax.ShapeDtypeStruct((b, ne, cap, d), BF16),
        scratch_shapes=[pltpu.VMEM((d, ff), BF16), pltpu.VMEM((d, ff), BF16), pltpu.VMEM((ff, d), BF16)],
        compiler_params=_params(("arbitrary", "arbitrary")),
        name="moe",
    )(xg, w_gate, w_up, w_down)


def _combine_kernel(start_ref, nh_ref, code_ref, p_ref, mod_ref, fg_ref, y_hbm, o_ref, ybuf, sem,
                    *, cap, n_tiles, n_steps):
    b = pl.program_id(0)
    k = pl.program_id(1)
    step = b * n_tiles + k
    tm = TOKEN_TILE
    w = SLOT_WINDOW

    def window(bb, kk, e, c):
        st = start_ref[(bb * N_EXPERTS + e) * LANES + kk]
        lo = _floor8(st) + c * w
        return lo, pl.multiple_of(jnp.minimum(lo, cap - w), SUBLANES)

    def copies(bb, kk, buf):
        out = []
        for e in range(N_EXPERTS):
            _, src = window(bb, kk, e, 0)
            out.append(pltpu.make_async_copy(y_hbm.at[bb, e, pl.ds(src, w), :], ybuf.at[buf, e], sem.at[buf, e]))
        return out

    def later(ahead):
        kk = k + ahead
        wrap = kk >= n_tiles
        return jnp.where(wrap, b + 1, b), jnp.where(wrap, kk - n_tiles, kk)

    @pl.when(step == 0)
    def _():
        for a in range(COMBINE_AHEAD):
            for cp in copies(*later(a), a):
                cp.start()

    cur = lax.rem(step, COMBINE_AHEAD + 1)

    @pl.when(step + COMBINE_AHEAD < n_steps)
    def _():
        for cp in copies(*later(COMBINE_AHEAD), lax.rem(step + COMBINE_AHEAD, COMBINE_AHEAD + 1)):
            cp.start()

    for cp in copies(b, k, cur):
        cp.wait()

    slot = lax.broadcasted_iota(I32, (w, tm), 0)
    p = p_ref[0]

    def scatter(c, buf_rows):
        weights = []
        for e in range(N_EXPERTS):
            lo, src = window(b, k, e, c)
            codes = code_ref[0, e:e + 1, :]
            hit = jnp.logical_and(codes == (slot + src), jnp.logical_and(codes >= lo, codes < lo + w))
            weights.append(jnp.where(hit, p[e:e + 1, :], 0.0).astype(BF16))
        return _dot_tn(jnp.concatenate(weights, axis=0), buf_rows)

    acc = scatter(0, ybuf[cur].reshape(N_EXPERTS * w, D_MODEL))

    most = jnp.int32(0)
    for e in range(N_EXPERTS):
        i0 = (b * N_EXPERTS + e) * LANES + k
        span = start_ref[i0 + 1] - _floor8(start_ref[i0])
        most = jnp.maximum(most, jnp.right_shift(span + w - 1, SLOT_SHIFT))

    def extra(c, acc):
        cps = []
        for e in range(N_EXPERTS):
            _, src = window(b, k, e, c)
            cps.append(pltpu.make_async_copy(y_hbm.at[b, e, pl.ds(src, w), :], ybuf.at[cur, e], sem.at[cur, e]))
        for cp in cps:
            cp.start()
        for cp in cps:
            cp.wait()
        return acc + scatter(c, ybuf[cur].reshape(N_EXPERTS * w, D_MODEL))

    acc = lax.fori_loop(1, most, extra, acc)
    g2 = mod_ref[pl.ds(b, 1), 5 * D_MODEL:6 * D_MODEL]
    o_ref[0] = _rms(nh_ref[0] + g2 * acc, fg_ref[...])


def _combine(starts_flat, new_hx, code, probs_t, mod, final_g, y, cap):
    b, l, d = new_hx.shape
    tm = TOKEN_TILE
    n_tiles = l // tm
    grid_spec = pltpu.PrefetchScalarGridSpec(
        num_scalar_prefetch=1,
        grid=(b, n_tiles),
        in_specs=[pl.BlockSpec((1, tm, d), lambda i, j, s: (i, j, 0)),
                  pl.BlockSpec((1, N_EXPERTS, tm), lambda i, j, s: (i, 0, j)),
                  pl.BlockSpec((1, N_EXPERTS, tm), lambda i, j, s: (i, 0, j)),
                  pl.BlockSpec(mod.shape, lambda i, j, s: (0, 0)),
                  pl.BlockSpec((1, d), lambda i, j, s: (0, 0)),
                  pl.BlockSpec(memory_space=pl.ANY)],
        out_specs=pl.BlockSpec((1, tm, d), lambda i, j, s: (i, j, 0)),
        scratch_shapes=[pltpu.VMEM((COMBINE_AHEAD + 1, N_EXPERTS, SLOT_WINDOW, d), BF16),
                        pltpu.SemaphoreType.DMA((COMBINE_AHEAD + 1, N_EXPERTS))],
    )
    return pl.pallas_call(
        functools.partial(_combine_kernel, cap=cap, n_tiles=n_tiles, n_steps=b * n_tiles),
        grid_spec=grid_spec,
        out_shape=jax.ShapeDtypeStruct((b, l, d), F32),
        compiler_params=_params(("arbitrary", "arbitrary")),
        name="combine",
    )(starts_flat, new_hx, code, probs_t, mod, final_g, y)


def _layer(x, c, ctx, c_ctx, lb_logits, w_mod, b_mod, norm1_g, w_in, hgrn_norm_g, w_a, w_pool, pool_scale,
           w_b, w_out, norm2_g, w_router, w_e_gate, w_e_up, w_e_down, final_g):
    b, l, d = x.shape
    cap = CAPACITY_FACTOR * l // N_EXPERTS
    assert b < MOD_ROWS and l % (HGRN_CHUNK * HGRN_STEP_CHUNKS) == 0 and l % PROJ_TILE == 0 and l % TOKEN_TILE == 0
    assert cap % SUBLANES == 0 and cap >= SLOT_WINDOW and l % (POOL_UNROLL * POOL_TILE) == 0
    cc = jnp.zeros((MOD_ROWS, d), F32).at[:b].set(c).at[b].set(c_ctx)
    assert lb_logits.shape == (2, 2, HGRN_WIDTH)
    lbl = lb_logits.reshape(4, HGRN_WIDTH)
    row = lambda a: a.reshape(1, -1)
    w_in_bf = w_in.astype(BF16)

    mod = _adaln(cc, w_mod, b_mod)
    s0f, s0b = _ctx_states(ctx, row(norm1_g), mod, lbl, w_in_bf)
    q, v, kf, gf, kb, gb, og, p, ga, gbm, safe = _in_proj(x, row(norm1_g), mod, lbl, w_in_bf)
    per_tile = TOKEN_TILE // HGRN_CHUNK
    safe = safe[:, :, :2 * per_tile, 0].reshape(b, -1, 2, per_tile)
    flags = safe.transpose(0, 1, 3, 2).astype(I32).reshape(-1)
    o_f, o_b = _hgrn(flags, q, v, kf, gf, kb, gb, s0f, s0b)
    pooled = _pool(p, w_pool.astype(BF16), pool_scale)
    new_hx, vx, probs_t = _merge(x, o_f, o_b, og, pooled, ga, gbm, mod, row(hgrn_norm_g), row(norm2_g),
                                 w_a.astype(BF16), w_b.astype(BF16), w_out.astype(BF16),
                                 w_router.T.astype(BF16))
    code, starts = _route(probs_t, cap)
    starts_flat = starts.reshape(-1)
    xg = _dispatch(starts_flat, vx, code, cap)
    y = _moe(xg, w_e_gate, w_e_up, w_e_down, cap)
    return _combine(starts_flat, new_hx, code, probs_t, mod, row(final_g), y, cap)


def kernel(x, c, ctx, c_ctx, lb_logits, w_mod, b_mod, norm1_g, w_in, hgrn_norm_g, w_a, w_pool, pool_scale,
           w_b, w_out, norm2_g, w_router, w_e_gate, w_e_up, w_e_down, final_g):
    assert w_mod.shape[0] == 1, "single-layer trunk: the context stream only seeds the latent recurrence"
    return _layer(x, c, ctx, c_ctx, lb_logits, w_mod[0], b_mod[0], norm1_g[0], w_in[0], hgrn_norm_g[0], w_a[0],
                  w_pool[0], pool_scale[0], w_b[0], w_out[0], norm2_g[0], w_router[0], w_e_gate[0],
                  w_e_up[0], w_e_down[0], final_g)
```
